```python
import jax, jax.numpy as jnp
from jax import lax
import numpy as np

D_MODEL = 2048
BATCH = 4
SEQ = 4096
DEPTH = 1

N_META = 16
CHUNK = 128
Q_BLOCK = 128
RET_HEADS = 8
RET_QK_DIM = 128
RET_V_DIM = 128
MLA_HEADS = 8
MLA_NOPE = 128
MLA_ROPE = 64
MLA_V = 128
MLA_Q_RANK = 512
MLA_KV_RANK = 256
D_MIX = RET_HEADS * RET_V_DIM + MLA_HEADS * MLA_V
IN_SIZES = (RET_HEADS * RET_QK_DIM, RET_HEADS * RET_QK_DIM, RET_HEADS * RET_V_DIM,
            RET_HEADS * RET_V_DIM, MLA_Q_RANK, MLA_KV_RANK, MLA_ROPE)
D_IN = sum(IN_SIZES)
D_FF = 5632
ROPE_THETA = 10000.0
EPS = 1e-6

kernel_name = "hymba_retnet_mla_macaron_sandwich"


def rmsnorm(x, w):
    x32 = x.astype(jnp.float32)
    y = x32 * lax.rsqrt(jnp.mean(x32 * x32, axis=-1, keepdims=True) + EPS)
    return (y * w.astype(jnp.float32)).astype(x.dtype)


def swiglu(h, w_gate, w_up, w_down):
    return (jax.nn.silu(h @ w_gate) * (h @ w_up)) @ w_down


def rope_tables(pos, dim):
    inv = ROPE_THETA ** (-jnp.arange(0, dim, 2, dtype=jnp.float32) / dim)
    ang = pos[:, None] * inv[None, :]
    return jnp.cos(ang), jnp.sin(ang)


def apply_rope(x, cos, sin):
    x32 = x.astype(jnp.float32)
    x1, x2 = jnp.split(x32, 2, axis=-1)
    out = jnp.concatenate([x1 * cos - x2 * sin, x2 * cos + x1 * sin], axis=-1)
    return out.astype(x.dtype)


def retention(q, k, v):
    b, seq_len, n_heads, dv = v.shape
    dk = q.shape[-1]
    log_g = jnp.log(1.0 - 2.0 ** (-5.0 - jnp.arange(n_heads, dtype=jnp.float32)))

    def decay_mask(n):
        idx = jnp.arange(n, dtype=jnp.float32)
        diff = idx[:, None] - idx[None, :]
        return jnp.where(diff[None] >= 0,
                         jnp.exp(jnp.maximum(diff, 0.0)[None] * log_g[:, None, None]), 0.0)

    def intra(qc, kc, vc, dmask):
        s = jnp.einsum('bnhd,bmhd->bhnm', qc, kc) * dmask[None]
        return jnp.einsum('bhnm,bmhv->bnhv', s, vc)

    qm, km, vm = q[:, :N_META], k[:, :N_META], v[:, :N_META]
    o_meta = intra(qm, km, vm, decay_mask(N_META))
    pos_m = jnp.arange(N_META, dtype=jnp.float32)
    zeta_m = jnp.exp((N_META - 1 - pos_m)[:, None] * log_g[None, :])
    state0 = jnp.einsum('bmhd,bmhv,mh->bhdv', km, vm, zeta_m)

    n_chunks = (seq_len - N_META) // CHUNK

    def to_chunks(t):
        return t[:, N_META:].reshape(b, n_chunks, CHUNK, n_heads, t.shape[-1]).transpose(1, 0, 2, 3, 4)

    dmask_c = decay_mask(CHUNK)
    pos_c = jnp.arange(CHUNK, dtype=jnp.float32)
    xi = jnp.exp((pos_c + 1.0)[:, None] * log_g[None, :])
    zeta = jnp.exp((CHUNK - 1 - pos_c)[:, None] * log_g[None, :])
    g_chunk = jnp.exp(CHUNK * log_g)

    def step(state, qkv):
        qc, kc, vc = qkv
        o = intra(qc, kc, vc, dmask_c) + \
            jnp.einsum('bnhd,bhdv->bnhv', qc, state) * xi[None, :, :, None]
        state = state * g_chunk[None, :, None, None] + \
            jnp.einsum('bmhd,bmhv,mh->bhdv', kc, vc, zeta)
        return state, o

    _, o_real = lax.scan(step, state0, (to_chunks(q), to_chunks(k), to_chunks(v)))
    o_real = o_real.transpose(1, 0, 2, 3, 4).reshape(b, n_chunks * CHUNK, n_heads, dv)
    return jnp.concatenate([o_meta, o_real], axis=1)


def mla_attention(q_nope, q_rope, k_nope, k_rope, v):
    b, seq_len, n_heads, dv = v.shape
    scale = (MLA_NOPE + MLA_ROPE) ** -0.5
    kpos = jnp.arange(seq_len, dtype=jnp.int32)

    def attend(qn, qr, qpos, kn, kr, vv, kp):
        s = (jnp.einsum('bqhd,bkhd->bhqk', qn, kn) +
             jnp.einsum('bqhr,bkr->bhqk', qr, kr)).astype(jnp.float32) * scale
        s = jnp.where(kp[None, :] <= qpos[:, None], s, -1e30)
        p = jax.nn.softmax(s, axis=-1)
        return jnp.einsum('bhqk,bkhv->bqhv', p.astype(vv.dtype), vv)

    o_meta = attend(q_nope[:, :N_META], q_rope[:, :N_META], kpos[:N_META],
                    k_nope[:, :N_META], k_rope[:, :N_META], v[:, :N_META], kpos[:N_META])

    n_blocks = (seq_len - N_META) // Q_BLOCK

    def blocks(t):
        return t[:, N_META:].reshape(b, n_blocks, Q_BLOCK, *t.shape[2:]).swapaxes(0, 1)

    qpos_blocks = (N_META + jnp.arange(seq_len - N_META, dtype=jnp.int32)).reshape(n_blocks, Q_BLOCK)
    o_real = lax.map(lambda a: attend(a[0], a[1], a[2], k_nope, k_rope, v, kpos),
                     (blocks(q_nope), blocks(q_rope), qpos_blocks))
    o_real = o_real.swapaxes(0, 1).reshape(b, seq_len - N_META, n_heads, dv)
    return jnp.concatenate([o_meta, o_real], axis=1)


def hybrid_mixer(u, w_in, ret_group_norm, mla_q_norm, mla_w_uq, mla_kv_norm,
                 mla_w_uk, mla_w_uv, w_out, cos_r, sin_r, cos_m, sin_m):
    b, seq_len, _ = u.shape
    proj = u @ w_in
    offsets = np.cumsum(IN_SIZES)[:-1].tolist()
    rq, rk, rv, rg, cq, ckv, kr = jnp.split(proj, offsets, axis=-1)

    rq = apply_rope(rq.reshape(b, seq_len, RET_HEADS, RET_QK_DIM), cos_r[:, None], sin_r[:, None])
    rk = apply_rope(rk.reshape(b, seq_len, RET_HEADS, RET_QK_DIM), cos_r[:, None], sin_r[:, None])
    rv = rv.reshape(b, seq_len, RET_HEADS, RET_V_DIM)
    ret = retention(rq.astype(jnp.float32),
                    rk.astype(jnp.float32) * (RET_QK_DIM ** -0.5),
                    rv.astype(jnp.float32))
    ret = ret * lax.rsqrt(jnp.mean(ret * ret, axis=-1, keepdims=True) + EPS)
    ret = ret.reshape(b, seq_len, RET_HEADS * RET_V_DIM) * ret_group_norm.astype(jnp.float32)
    ret = (jax.nn.silu(rg.astype(jnp.float32)) * ret).astype(u.dtype)

    cq = rmsnorm(cq, mla_q_norm)
    q = (cq @ mla_w_uq).reshape(b, seq_len, MLA_HEADS, MLA_NOPE + MLA_ROPE)
    q_nope, q_rope = q[..., :MLA_NOPE], q[..., MLA_NOPE:]
    q_rope = apply_rope(q_rope, cos_m[:, None], sin_m[:, None])
    ckv = rmsnorm(ckv, mla_kv_norm)
    k_nope = (ckv @ mla_w_uk).reshape(b, seq_len, MLA_HEADS, MLA_NOPE)
    v = (ckv @ mla_w_uv).reshape(b, seq_len, MLA_HEADS, MLA_V)
    k_rope = apply_rope(kr, cos_m, sin_m)
    mla = mla_attention(q_nope, q_rope, k_nope, k_rope, v).reshape(b, seq_len, MLA_HEADS * MLA_V)

    return jnp.concatenate([ret, mla.astype(u.dtype)], axis=-1) @ w_out


def _normal(k, shape, fan_in):
    return jax.random.normal(k, shape, jnp.float32) * (fan_in ** -0.5)


def _gain(k, shape):
    return 1.0 + 0.02 * jax.random.normal(k, shape, jnp.float32)


def setup_inputs(seed: int = 0) -> dict:
    key = jax.random.key(seed)
    ks = jax.random.split(key, 24)
    L = DEPTH
    return {
        "x": jax.random.normal(ks[0], (BATCH, SEQ, D_MODEL), jnp.float32),
        "meta_tokens": jax.random.normal(ks[1], (N_META, D_MODEL), jnp.float32),
        "ffn1_pre_norm": _gain(ks[2], (L, D_MODEL)),
        "ffn1_w_gate": _normal(ks[3], (L, D_MODEL, D_FF), D_MODEL),
        "ffn1_w_up": _normal(ks[4], (L, D_MODEL, D_FF), D_MODEL),
        "ffn1_w_down": _normal(ks[5], (L, D_FF, D_MODEL), D_FF),
        "ffn1_post_norm": _gain(ks[6], (L, D_MODEL)),
        "mix_pre_norm": _gain(ks[7], (L, D_MODEL)),
        "w_in": _normal(ks[8], (L, D_MODEL, D_IN), D_MODEL),
        "ret_group_norm": _gain(ks[9], (L, RET_HEADS * RET_V_DIM)),
        "mla_q_norm": _gain(ks[10], (L, MLA_Q_RANK)),
        "mla_w_uq": _normal(ks[11], (L, MLA_Q_RANK, MLA_HEADS * (MLA_NOPE + MLA_ROPE)), MLA_Q_RANK),
        "mla_kv_norm": _gain(ks[12], (L, MLA_KV_RANK)),
        "mla_w_uk": _normal(ks[13], (L, MLA_KV_RANK, MLA_HEADS * MLA_NOPE), MLA_KV_RANK),
        "mla_w_uv": _normal(ks[14], (L, MLA_KV_RANK, MLA_HEADS * MLA_V), MLA_KV_RANK),
        "w_out": _normal(ks[15], (L, D_MIX, D_MODEL), D_MIX),
        "mix_post_norm": _gain(ks[16], (L, D_MODEL)),
        "ffn2_pre_norm": _gain(ks[17], (L, D_MODEL)),
        "ffn2_w_gate": _normal(ks[18], (L, D_MODEL, D_FF), D_MODEL),
        "ffn2_w_up": _normal(ks[19], (L, D_MODEL, D_FF), D_MODEL),
        "ffn2_w_down": _normal(ks[20], (L, D_FF, D_MODEL), D_FF),
        "ffn2_post_norm": _gain(ks[21], (L, D_MODEL)),
    }


def reference(x, meta_tokens, ffn1_pre_norm, ffn1_w_gate, ffn1_w_up, ffn1_w_down,
              ffn1_post_norm, mix_pre_norm, w_in, ret_group_norm, mla_q_norm, mla_w_uq,
              mla_kv_norm, mla_w_uk, mla_w_uv, w_out, mix_post_norm, ffn2_pre_norm,
              ffn2_w_gate, ffn2_w_up, ffn2_w_down, ffn2_post_norm):
    b = x.shape[0]
    meta = jnp.broadcast_to(meta_tokens.astype(x.dtype)[None], (b, N_META, x.shape[-1]))
    h = jnp.concatenate([meta, x], axis=1)
    seq_len = h.shape[1]
    pos = jnp.arange(seq_len, dtype=jnp.float32)
    cos_r, sin_r = rope_tables(pos, RET_QK_DIM)
    cos_m, sin_m = rope_tables(pos, MLA_ROPE)
    for l in range(DEPTH):
        f = swiglu(rmsnorm(h, ffn1_pre_norm[l]), ffn1_w_gate[l], ffn1_w_up[l], ffn1_w_down[l])
        h = h + 0.5 * rmsnorm(f, ffn1_post_norm[l])
        m = hybrid_mixer(rmsnorm(h, mix_pre_norm[l]), w_in[l], ret_group_norm[l], mla_q_norm[l],
                         mla_w_uq[l], mla_kv_norm[l], mla_w_uk[l], mla_w_uv[l], w_out[l],
                         cos_r, sin_r, cos_m, sin_m)
        h = h + rmsnorm(m, mix_post_norm[l])
        f = swiglu(rmsnorm(h, ffn2_pre_norm[l]), ffn2_w_gate[l], ffn2_w_up[l], ffn2_w_down[l])
        h = h + 0.5 * rmsnorm(f, ffn2_post_norm[l])
    return h[:, N_META:]
```

```python
import functools

import jax
import jax.numpy as jnp
from jax import lax
from jax.experimental import pallas as pl
from jax.experimental.pallas import tpu as pltpu

D_MODEL = 2048
N_META = 16
CHUNK = 128
RET_HEADS = 8
RET_DIM = 128
MLA_HEADS = 8
MLA_NOPE = 128
MLA_ROPE = 64
MLA_V = 128
MLA_Q_RANK = 512
MLA_KV_RANK = 256
MLA_QK_PAD = 256
D_RET = RET_HEADS * RET_DIM
D_MLA = MLA_HEADS * MLA_V
D_FF = 5632
ROPE_THETA = 10000.0
EPS = 1e-6
LANES = 128

_OFF_RQ = 0
_OFF_RK = D_RET
_OFF_RV = 2 * D_RET
_OFF_RG = 3 * D_RET
_OFF_CQ = 4 * D_RET
_OFF_CKV = _OFF_CQ + MLA_Q_RANK
_OFF_KR = _OFF_CKV + MLA_KV_RANK
D_IN_PAD = _OFF_KR + LANES

VMEM_LIMIT = 60 * 1024 * 1024

f32 = jnp.float32
bf16 = jnp.bfloat16


def _rms(x, w):
    return x * lax.rsqrt(jnp.mean(x * x, axis=-1, keepdims=True) + EPS) * w


def _dot(a, b):
    return jnp.dot(a, b, preferred_element_type=f32)


def _dot_nt(a, b):
    return lax.dot_general(a, b, (((1,), (1,)), ((), ())), preferred_element_type=f32)


def _dot_tn(a, b):
    return lax.dot_general(a, b, (((0,), (0,)), ((), ())), preferred_element_type=f32)


def _resident(shape):
    nd = len(shape)
    return pl.BlockSpec(shape, lambda *_: (0,) * nd, pipeline_mode=pl.Buffered(1))


def _ffn_body(h_ref, pre_ref, wg_ref, wu_ref, wd_ref, post_ref, o_ref, xn_ref, acc_ref):
    j = pl.program_id(1)

    @pl.when(j == 0)
    def _():
        xn_ref[...] = _rms(h_ref[...], pre_ref[...]).astype(bf16)
        acc_ref[...] = jnp.zeros_like(acc_ref)

    xn = xn_ref[...]
    g = _dot(xn, wg_ref[...])
    u = _dot(xn, wu_ref[...])
    a = (g * jax.nn.sigmoid(g) * u).astype(bf16)
    acc_ref[...] += _dot(a, wd_ref[...])

    @pl.when(j == pl.num_programs(1) - 1)
    def _():
        o_ref[...] = h_ref[...] + 0.5 * _rms(acc_ref[...], post_ref[...])


def _ffn(h, pre_w, wg, wu, wd, post_w, *, tm, tf):
    n = h.shape[0]
    row = pl.BlockSpec((tm, D_MODEL), lambda i, j: (i, 0))
    vec = pl.BlockSpec((1, D_MODEL), lambda i, j: (0, 0))
    return pl.pallas_call(
        _ffn_body,
        grid=(n // tm, D_FF // tf),
        in_specs=[
            row,
            vec,
            pl.BlockSpec((D_MODEL, tf), lambda i, j: (0, j)),
            pl.BlockSpec((D_MODEL, tf), lambda i, j: (0, j)),
            pl.BlockSpec((tf, D_MODEL), lambda i, j: (j, 0)),
            vec,
        ],
        out_specs=row,
        out_shape=jax.ShapeDtypeStruct((n, D_MODEL), f32),
        scratch_shapes=[pltpu.VMEM((tm, D_MODEL), bf16), pltpu.VMEM((tm, D_MODEL), f32)],
        compiler_params=pltpu.CompilerParams(
            dimension_semantics=("parallel", "arbitrary"), vmem_limit_bytes=VMEM_LIMIT),
        name="swiglu",
    )(h, pre_w, wg, wu, wd, post_w)


def _rot(x, c, s):
    return x * c + pltpu.roll(x, 64, 1) * s


def _mix_in_body(h_ref, pre_ref, win_ref, qn_ref, wuq_ref, kvn_ref, wuk_ref, wuv_ref,
                 cr_ref, sr_ref, cm_ref, sm_ref,
                 rq_ref, rk_ref, rv_ref, rg_ref, qc_ref, kc_ref, v_ref):
    xn = _rms(h_ref[...], pre_ref[...]).astype(bf16)
    proj = _dot(xn, win_ref[...])
    cr, sr = cr_ref[...], sr_ref[...]
    cm, sm = cm_ref[...], sm_ref[...]
    k_scale = RET_DIM ** -0.5
    for hd in range(RET_HEADS):
        lo = hd * RET_DIM
        sl = slice(lo, lo + RET_DIM)
        rq_ref[:, sl] = _rot(proj[:, _OFF_RQ + lo:_OFF_RQ + lo + RET_DIM], cr, sr).astype(bf16)
        rk = _rot(proj[:, _OFF_RK + lo:_OFF_RK + lo + RET_DIM], cr, sr)
        rk_ref[:, sl] = (rk * k_scale).astype(bf16)
    rv_ref[...] = proj[:, _OFF_RV:_OFF_RV + D_RET].astype(bf16)
    rg_ref[...] = proj[:, _OFF_RG:_OFF_RG + D_RET]

    cq = _rms(proj[:, _OFF_CQ:_OFF_CQ + MLA_Q_RANK], qn_ref[...]).astype(bf16)
    q = _dot(cq, wuq_ref[...])
    ckv = _rms(proj[:, _OFF_CKV:_OFF_CKV + MLA_KV_RANK], kvn_ref[...]).astype(bf16)
    kn = _dot(ckv, wuk_ref[...])
    v_ref[...] = _dot(ckv, wuv_ref[...]).astype(bf16)
    kr = _rot(proj[:, _OFF_KR:_OFF_KR + LANES], cm, sm).astype(bf16)
    for hd in range(MLA_HEADS):
        lo = hd * MLA_QK_PAD
        qc_ref[:, lo:lo + MLA_NOPE] = q[:, lo:lo + MLA_NOPE].astype(bf16)
        qc_ref[:, lo + MLA_NOPE:lo + MLA_QK_PAD] = _rot(
            q[:, lo + MLA_NOPE:lo + MLA_QK_PAD], cm, sm).astype(bf16)
        kc_ref[:, lo:lo + MLA_NOPE] = kn[:, hd * MLA_NOPE:(hd + 1) * MLA_NOPE].astype(bf16)
        kc_ref[:, lo + MLA_NOPE:lo + MLA_QK_PAD] = kr


def _mix_in(h, pre_w, win, qn_w, wuq, kvn_w, wuk, wuv, cr, sr, cm, sm, *, tm):
    n = h.shape[0]
    n_pos_blocks = cr.shape[0] // tm

    def row(width):
        return pl.BlockSpec((tm, width), lambda i: (i, 0))

    tab = pl.BlockSpec((tm, LANES), lambda i: (i % n_pos_blocks, 0))
    d_qk = MLA_HEADS * MLA_QK_PAD
    out_shape = [
        jax.ShapeDtypeStruct((n, D_RET), bf16),
        jax.ShapeDtypeStruct((n, D_RET), bf16),
        jax.ShapeDtypeStruct((n, D_RET), bf16),
        jax.ShapeDtypeStruct((n, D_RET), f32),
        jax.ShapeDtypeStruct((n, d_qk), bf16),
        jax.ShapeDtypeStruct((n, d_qk), bf16),
        jax.ShapeDtypeStruct((n, D_MLA), bf16),
    ]
    return pl.pallas_call(
        _mix_in_body,
        grid=(n // tm,),
        in_specs=[
            row(D_MODEL),
            _resident((1, D_MODEL)),
            _resident(win.shape),
            _resident((1, MLA_Q_RANK)),
            _resident(wuq.shape),
            _resident((1, MLA_KV_RANK)),
            _resident(wuk.shape),
            _resident(wuv.shape),
            tab, tab, tab, tab,
        ],
        out_specs=[row(D_RET), row(D_RET), row(D_RET), row(D_RET), row(d_qk), row(d_qk),
                   row(D_MLA)],
        out_shape=out_shape,
        compiler_params=pltpu.CompilerParams(
            dimension_semantics=("parallel",), vmem_limit_bytes=VMEM_LIMIT),
        name="mix_in",
    )(h, pre_w, win, qn_w, wuq, kvn_w, wuk, wuv, cr, sr, cm, sm)


def _retention_body(lg_ref, rq_ref, rk_ref, rv_ref, rg_ref, gn_ref, km_ref, vm_ref, o_ref,
                    *, n_chunks):
    lg = lg_ref[0]
    row = lax.broadcasted_iota(jnp.int32, (CHUNK, CHUNK), 0).astype(f32)
    col = lax.broadcasted_iota(jnp.int32, (CHUNK, CHUNK), 1).astype(f32)
    diff = row - col
    dmask = jnp.where(diff >= 0, jnp.exp(jnp.maximum(diff, 0.0) * lg), 0.0)
    xi = jnp.exp((row + 1.0) * lg)
    zeta = jnp.exp((CHUNK - 1.0 - row) * lg)
    g_chunk = jnp.exp(CHUNK * lg)
    gn = gn_ref[...]

    mrow = lax.broadcasted_iota(jnp.int32, (N_META, RET_DIM), 0).astype(f32)
    zeta_m = jnp.exp((N_META - 1.0 - mrow) * lg)
    state0 = _dot_tn((km_ref[...].astype(f32) * zeta_m).astype(bf16), vm_ref[...])

    def step(c, state):
        r = pl.ds(pl.multiple_of(c * CHUNK, CHUNK), CHUNK)
        q, k, v = rq_ref[r, :], rk_ref[r, :], rv_ref[r, :]
        s = _dot_nt(q, k) * dmask
        o = _dot(s.astype(bf16), v) + _dot(q, state.astype(bf16)) * xi
        state = state * g_chunk + _dot_tn((k.astype(f32) * zeta).astype(bf16), v)
        g = rg_ref[r, :]
        y = _rms(o, gn) * (g * jax.nn.sigmoid(g))
        o_ref[r, :] = y.astype(bf16)
        return state

    lax.fori_loop(0, n_chunks, step, state0)


def _retention(log_g, rq, rk, rv, rg, gn_w, rk_meta, rv_meta, *, batch, seq):
    n_chunks = seq // CHUNK
    blk = pl.BlockSpec((seq, RET_DIM), lambda b, h: (b, h))
    meta = pl.BlockSpec((N_META, RET_DIM), lambda b, h: (0, h))
    return pl.pallas_call(
        functools.partial(_retention_body, n_chunks=n_chunks),
        grid=(batch, RET_HEADS),
        in_specs=[
            pl.BlockSpec((1, 1, LANES), lambda b, h: (h, 0, 0)),
            blk, blk, blk, blk,
            pl.BlockSpec((1, RET_DIM), lambda b, h: (0, h)),
            meta, meta,
        ],
        out_specs=blk,
        out_shape=jax.ShapeDtypeStruct((batch * seq, D_RET), bf16),
        compiler_params=pltpu.CompilerParams(
            dimension_semantics=("parallel", "parallel"), vmem_limit_bytes=VMEM_LIMIT),
        name="retention",
    )(log_g, rq, rk, rv, rg, gn_w, rk_meta, rv_meta)


def _mla_body(q_ref, k_ref, v_ref, km_ref, vm_ref, o_ref, m_ref, l_ref, acc_ref, *, tq):
    qi = pl.program_id(2)
    scale = (MLA_NOPE + MLA_ROPE) ** -0.5
    q = q_ref[...]

    s = _dot_nt(q, km_ref[...]) * scale
    m0 = jnp.max(s, axis=-1, keepdims=True)
    p = jnp.exp(s - m0)
    m_ref[...] = m0
    l_ref[...] = jnp.sum(p, axis=-1, keepdims=True)
    acc_ref[...] = _dot(p.astype(bf16), vm_ref[...])

    def block(j, masked):
        r = pl.ds(pl.multiple_of(j * tq, tq), tq)
        s = _dot_nt(q, k_ref[r, :]) * scale
        if masked:
            qpos = lax.broadcasted_iota(jnp.int32, (tq, tq), 0)
            kpos = lax.broadcasted_iota(jnp.int32, (tq, tq), 1)
            s = jnp.where(kpos <= qpos, s, -1e30)
        m_old = m_ref[...]
        m_new = jnp.maximum(m_old, jnp.max(s, axis=-1, keepdims=True))
        alpha = jnp.exp(m_old - m_new)
        p = jnp.exp(s - m_new)
        l_ref[...] = alpha * l_ref[...] + jnp.sum(p, axis=-1, keepdims=True)
        acc_ref[...] = alpha * acc_ref[...] + _dot(p.astype(bf16), v_ref[r, :])
        m_ref[...] = m_new

    def body(j, carry):
        block(j, masked=False)
        return carry

    lax.fori_loop(0, qi, body, 0)
    block(qi, masked=True)
    o_ref[...] = (acc_ref[...] / l_ref[...]).astype(bf16)


def _mla(qc, kc, v, kc_meta, v_meta, *, batch, seq, tq):
    nq = seq // tq
    return pl.pallas_call(
        functools.partial(_mla_body, tq=tq),
        grid=(batch, MLA_HEADS, nq),
        in_specs=[
            pl.BlockSpec((tq, MLA_QK_PAD), lambda b, h, i: (b * nq + i, h)),
            pl.BlockSpec((seq, MLA_QK_PAD), lambda b, h, i: (b, h)),
            pl.BlockSpec((seq, MLA_V), lambda b, h, i: (b, h)),
            pl.BlockSpec((N_META, MLA_QK_PAD), lambda b, h, i: (0, h)),
            pl.BlockSpec((N_META, MLA_V), lambda b, h, i: (0, h)),
        ],
        out_specs=pl.BlockSpec((tq, MLA_V), lambda b, h, i: (b * nq + i, h)),
        out_shape=jax.ShapeDtypeStruct((batch * seq, D_MLA), bf16),
        scratch_shapes=[pltpu.VMEM((tq, 1), f32), pltpu.VMEM((tq, 1), f32),
                        pltpu.VMEM((tq, MLA_V), f32)],
        compiler_params=pltpu.CompilerParams(
            dimension_semantics=("parallel", "parallel", "arbitrary"),
            vmem_limit_bytes=VMEM_LIMIT),
        name="mla_attention",
    )(qc, kc, v, kc_meta, v_meta)


def _mix_out_body(ret_ref, mla_ref, wa_ref, wb_ref, h_ref, post_ref, o_ref):
    m = _dot(ret_ref[...], wa_ref[...]) + _dot(mla_ref[...], wb_ref[...])
    o_ref[...] = h_ref[...] + _rms(m, post_ref[...])


def _mix_out(ret, mla, wo_ret, wo_mla, h, post_w, *, tm):
    n = h.shape[0]
    return pl.pallas_call(
        _mix_out_body,
        grid=(n // tm,),
        in_specs=[
            pl.BlockSpec((tm, D_RET), lambda i: (i, 0)),
            pl.BlockSpec((tm, D_MLA), lambda i: (i, 0)),
            _resident(wo_ret.shape),
            _resident(wo_mla.shape),
            pl.BlockSpec((tm, D_MODEL), lambda i: (i, 0)),
            _resident((1, D_MODEL)),
        ],
        out_specs=pl.BlockSpec((tm, D_MODEL), lambda i: (i, 0)),
        out_shape=jax.ShapeDtypeStruct((n, D_MODEL), f32),
        compiler_params=pltpu.CompilerParams(
            dimension_semantics=("parallel",), vmem_limit_bytes=VMEM_LIMIT),
        name="mix_out",
    )(ret, mla, wo_ret, wo_mla, h, post_w)


def _rope_tables(seq_len):
    pos = jnp.arange(seq_len, dtype=f32)

    def cos_sin(dim):
        inv = ROPE_THETA ** (-jnp.arange(0, dim, 2, dtype=f32) / dim)
        ang = pos[:, None] * inv[None, :]
        return jnp.cos(ang), jnp.sin(ang)

    c, s = cos_sin(RET_DIM)
    cr = jnp.concatenate([c, c], axis=-1)
    sr = jnp.concatenate([-s, s], axis=-1)
    c, s = cos_sin(MLA_ROPE)
    z = jnp.zeros_like(c)
    cm = jnp.concatenate([c, z, c, z], axis=-1)
    sm = jnp.concatenate([-s, z, s, z], axis=-1)
    return cr, sr, cm, sm


def _pad_rope_cols(w):
    half = MLA_ROPE // 2
    z = jnp.zeros(w.shape[:-1] + (half,), w.dtype)
    return jnp.concatenate([w[..., :half], z, w[..., half:], z], axis=-1)


def kernel(x, meta_tokens, ffn1_pre_norm, ffn1_w_gate, ffn1_w_up, ffn1_w_down, ffn1_post_norm, mix_pre_norm, w_in, ret_group_norm, mla_q_norm, mla_w_uq, mla_kv_norm, mla_w_uk, mla_w_uv, w_out, mix_post_norm, ffn2_pre_norm, ffn2_w_gate, ffn2_w_up, ffn2_w_down, ffn2_post_norm):
    batch, seq, _ = x.shape
    assert ffn1_pre_norm.shape[0] == 1, "single-layer trunk only"
    l = 0
    cr, sr, cm, sm = _rope_tables(N_META + seq)
    tabs_meta = tuple(t[:N_META] for t in (cr, sr, cm, sm))
    tabs_real = tuple(t[N_META:] for t in (cr, sr, cm, sm))
    log_g = jnp.log(1.0 - 2.0 ** (-5.0 - jnp.arange(RET_HEADS, dtype=f32)))
    log_g = jnp.broadcast_to(log_g[:, None, None], (RET_HEADS, 1, LANES))

    h = x.reshape(batch * seq, D_MODEL)
    hm = meta_tokens.astype(x.dtype)
    vec = lambda w: w[l].reshape(1, -1)
    ffn1 = (vec(ffn1_pre_norm), ffn1_w_gate[l].astype(bf16), ffn1_w_up[l].astype(bf16),
            ffn1_w_down[l].astype(bf16), vec(ffn1_post_norm))
    ffn2 = (vec(ffn2_pre_norm), ffn2_w_gate[l].astype(bf16), ffn2_w_up[l].astype(bf16),
            ffn2_w_down[l].astype(bf16), vec(ffn2_post_norm))
    win = jnp.concatenate(
        [w_in[l][:, :_OFF_KR], _pad_rope_cols(w_in[l][:, _OFF_KR:])], axis=-1).astype(bf16)
    wuq = mla_w_uq[l].reshape(MLA_Q_RANK, MLA_HEADS, MLA_NOPE + MLA_ROPE)
    wuq = jnp.concatenate([wuq[..., :MLA_NOPE], _pad_rope_cols(wuq[..., MLA_NOPE:])], axis=-1)
    wuq = wuq.reshape(MLA_Q_RANK, MLA_HEADS * MLA_QK_PAD).astype(bf16)
    mix_w = (vec(mix_pre_norm), win, vec(mla_q_norm), wuq, vec(mla_kv_norm),
             mla_w_uk[l].astype(bf16), mla_w_uv[l].astype(bf16))
    wo = w_out[l].astype(bf16)

    h = _ffn(h, *ffn1, tm=512, tf=512)
    hm = _ffn(hm, *ffn1, tm=N_META, tf=512)
    _, rk_m, rv_m, _, _, kc_m, v_m = _mix_in(hm, *mix_w, *tabs_meta, tm=N_META)
    rq, rk, rv, rg, qc, kc, v = _mix_in(h, *mix_w, *tabs_real, tm=256)
    ret = _retention(log_g, rq, rk, rv, rg, vec(ret_group_norm), rk_m, rv_m,
                     batch=batch, seq=seq)
    mla = _mla(qc, kc, v, kc_m, v_m, batch=batch, seq=seq, tq=512)
    h = _mix_out(ret, mla, wo[:D_RET], wo[D_RET:], h, vec(mix_post_norm), tm=512)
    h = _ffn(h, *ffn2, tm=512, tf=512)
    return h.reshape(batch, seq, D_MODEL)
```

```python
import functools

import jax
import jax.numpy as jnp
from jax import lax
from jax.experimental import pallas as pl
from jax.experimental.pallas import tpu as pltpu

D_MODEL = 2048
N_META = 16
CHUNK = 128
RET_HEADS = 8
RET_DIM = 128
MLA_HEADS = 8
MLA_NOPE = 128
MLA_ROPE = 64
MLA_V = 128
MLA_Q_RANK = 512
MLA_KV_RANK = 256
MLA_QK_PAD = 256
D_RET = RET_HEADS * RET_DIM
D_MLA = MLA_HEADS * MLA_V
D_FF = 5632
ROPE_THETA = 10000.0
EPS = 1e-6
LANES = 128

_OFF_RQ = 0
_OFF_RK = D_RET
_OFF_RV = 2 * D_RET
_OFF_RG = 3 * D_RET
_OFF_CQ = 4 * D_RET
_OFF_CKV = _OFF_CQ + MLA_Q_RANK
_OFF_KR = _OFF_CKV + MLA_KV_RANK
D_IN_PAD = _OFF_KR + LANES

MLA_Q_SCALE = (MLA_NOPE + MLA_ROPE) ** -0.5 * 1.4426950408889634

VMEM_LIMIT = 60 * 1024 * 1024

f32 = jnp.float32
bf16 = jnp.bfloat16


def _rms(x, w):
    return x * lax.rsqrt(jnp.mean(x * x, axis=-1, keepdims=True) + EPS) * w


def _dot(a, b):
    return jnp.dot(a, b, preferred_element_type=f32)


def _dot_nt(a, b):
    return lax.dot_general(a, b, (((1,), (1,)), ((), ())), preferred_element_type=f32)


def _dot_tn(a, b):
    return lax.dot_general(a, b, (((0,), (0,)), ((), ())), preferred_element_type=f32)


def _resident(shape):
    nd = len(shape)
    return pl.BlockSpec(shape, lambda *_: (0,) * nd, pipeline_mode=pl.Buffered(1))


def _ffn_body(h_ref, pre_ref, wg_ref, wu_ref, wd_ref, post_ref, o_ref, xn_ref, acc_ref):
    j = pl.program_id(1)

    @pl.when(j == 0)
    def _():
        xn_ref[...] = _rms(h_ref[...], pre_ref[...]).astype(bf16)
        acc_ref[...] = jnp.zeros_like(acc_ref)

    xn = xn_ref[...]
    g = _dot(xn, wg_ref[...])
    u = _dot(xn, wu_ref[...])
    a = (g * jax.nn.sigmoid(g) * u).astype(bf16)
    acc_ref[...] += _dot(a, wd_ref[...])

    @pl.when(j == pl.num_programs(1) - 1)
    def _():
        o_ref[...] = h_ref[...] + 0.5 * _rms(acc_ref[...], post_ref[...])


def _ffn(h, pre_w, wg, wu, wd, post_w, *, tm, tf):
    n = h.shape[0]
    row = pl.BlockSpec((tm, D_MODEL), lambda i, j: (i, 0))
    vec = pl.BlockSpec((1, D_MODEL), lambda i, j: (0, 0))
    return pl.pallas_call(
        _ffn_body,
        grid=(n // tm, D_FF // tf),
        in_specs=[
            row,
            vec,
            pl.BlockSpec((D_MODEL, tf), lambda i, j: (0, j)),
            pl.BlockSpec((D_MODEL, tf), lambda i, j: (0, j)),
            pl.BlockSpec((tf, D_MODEL), lambda i, j: (j, 0)),
            vec,
        ],
        out_specs=row,
        out_shape=jax.ShapeDtypeStruct((n, D_MODEL), f32),
        scratch_shapes=[pltpu.VMEM((tm, D_MODEL), bf16), pltpu.VMEM((tm, D_MODEL), f32)],
        compiler_params=pltpu.CompilerParams(
            dimension_semantics=("parallel", "arbitrary"), vmem_limit_bytes=VMEM_LIMIT),
        name="swiglu",
    )(h, pre_w, wg, wu, wd, post_w)


def _rot(x, c, s):
    return x * c + pltpu.roll(x, 64, 1) * s


def _mix_in_body(h_ref, pre_ref, win_ref, qn_ref, wuq_ref, kvn_ref, wuk_ref, wuv_ref,
                 cr_ref, sr_ref, cm_ref, sm_ref,
                 rq_ref, rk_ref, rv_ref, rg_ref, qc_ref, kc_ref, v_ref):
    xn = _rms(h_ref[...], pre_ref[...]).astype(bf16)
    proj = _dot(xn, win_ref[...])
    cr, sr = cr_ref[...], sr_ref[...]
    cm, sm = cm_ref[...], sm_ref[...]
    k_scale = RET_DIM ** -0.5
    for hd in range(RET_HEADS):
        lo = hd * RET_DIM
        sl = slice(lo, lo + RET_DIM)
        rq_ref[:, sl] = _rot(proj[:, _OFF_RQ + lo:_OFF_RQ + lo + RET_DIM], cr, sr).astype(bf16)
        rk = _rot(proj[:, _OFF_RK + lo:_OFF_RK + lo + RET_DIM], cr, sr)
        rk_ref[:, sl] = (rk * k_scale).astype(bf16)
    rv_ref[...] = proj[:, _OFF_RV:_OFF_RV + D_RET].astype(bf16)
    rg_ref[...] = proj[:, _OFF_RG:_OFF_RG + D_RET]

    cq = _rms(proj[:, _OFF_CQ:_OFF_CQ + MLA_Q_RANK], qn_ref[...]).astype(bf16)
    q = _dot(cq, wuq_ref[...])
    ckv = _rms(proj[:, _OFF_CKV:_OFF_CKV + MLA_KV_RANK], kvn_ref[...]).astype(bf16)
    kn = _dot(ckv, wuk_ref[...])
    v_ref[...] = _dot(ckv, wuv_ref[...]).astype(bf16)
    kr = _rot(proj[:, _OFF_KR:_OFF_KR + LANES], cm, sm).astype(bf16)
    for hd in range(MLA_HEADS):
        lo = hd * MLA_QK_PAD
        qc_ref[:, lo:lo + MLA_NOPE] = (q[:, lo:lo + MLA_NOPE] * MLA_Q_SCALE).astype(bf16)
        qc_ref[:, lo + MLA_NOPE:lo + MLA_QK_PAD] = (_rot(
            q[:, lo + MLA_NOPE:lo + MLA_QK_PAD], cm, sm) * MLA_Q_SCALE).astype(bf16)
        kc_ref[:, lo:lo + MLA_NOPE] = kn[:, hd * MLA_NOPE:(hd + 1) * MLA_NOPE].astype(bf16)
        kc_ref[:, lo + MLA_NOPE:lo + MLA_QK_PAD] = kr


def _mix_in(h, pre_w, win, qn_w, wuq, kvn_w, wuk, wuv, cr, sr, cm, sm, *, tm):
    n = h.shape[0]
    n_pos_blocks = cr.shape[0] // tm

    def row(width):
        return pl.BlockSpec((tm, width), lambda i: (i, 0))

    tab = pl.BlockSpec((tm, LANES), lambda i: (i % n_pos_blocks, 0))
    d_qk = MLA_HEADS * MLA_QK_PAD
    out_shape = [
        jax.ShapeDtypeStruct((n, D_RET), bf16),
        jax.ShapeDtypeStruct((n, D_RET), bf16),
        jax.ShapeDtypeStruct((n, D_RET), bf16),
        jax.ShapeDtypeStruct((n, D_RET), f32),
        jax.ShapeDtypeStruct((n, d_qk), bf16),
        jax.ShapeDtypeStruct((n, d_qk), bf16),
        jax.ShapeDtypeStruct((n, D_MLA), bf16),
    ]
    return pl.pallas_call(
        _mix_in_body,
        grid=(n // tm,),
        in_specs=[
            row(D_MODEL),
            _resident((1, D_MODEL)),
            _resident(win.shape),
            _resident((1, MLA_Q_RANK)),
            _resident(wuq.shape),
            _resident((1, MLA_KV_RANK)),
            _resident(wuk.shape),
            _resident(wuv.shape),
            tab, tab, tab, tab,
        ],
        out_specs=[row(D_RET), row(D_RET), row(D_RET), row(D_RET), row(d_qk), row(d_qk),
                   row(D_MLA)],
        out_shape=out_shape,
        compiler_params=pltpu.CompilerParams(
            dimension_semantics=("parallel",), vmem_limit_bytes=VMEM_LIMIT),
        name="mix_in",
    )(h, pre_w, win, qn_w, wuq, kvn_w, wuk, wuv, cr, sr, cm, sm)


def _retention_body(lg_ref, rq_ref, rk_ref, rv_ref, rg_ref, gn_ref, km_ref, vm_ref, o_ref,
                    dmask_ref, xi_ref, zeta_ref, state_ref, *, rows):
    t = pl.program_id(1)

    @pl.when(t == 0)
    def _():
        row = lax.broadcasted_iota(jnp.int32, (CHUNK, CHUNK), 0).astype(f32)
        col = lax.broadcasted_iota(jnp.int32, (CHUNK, CHUNK), 1).astype(f32)
        diff = row - col
        mrow = lax.broadcasted_iota(jnp.int32, (N_META, RET_DIM), 0).astype(f32)
        for hd in range(RET_HEADS):
            lg = lg_ref[hd]
            sl = slice(hd * RET_DIM, (hd + 1) * RET_DIM)
            dmask_ref[hd] = jnp.where(diff >= 0, jnp.exp(jnp.maximum(diff, 0.0) * lg), 0.0)
            xi_ref[hd] = jnp.exp((row + 1.0) * lg)
            zeta_ref[hd] = jnp.exp((CHUNK - 1.0 - row) * lg)
            zeta_m = jnp.exp((N_META - 1.0 - mrow) * lg)
            state_ref[hd] = _dot_tn((km_ref[:, sl].astype(f32) * zeta_m).astype(bf16),
                                    vm_ref[:, sl])

    for hd in range(RET_HEADS):
        sl = slice(hd * RET_DIM, (hd + 1) * RET_DIM)
        g_chunk = jnp.exp(CHUNK * lg_ref[hd])
        dmask, xi, zeta = dmask_ref[hd], xi_ref[hd], zeta_ref[hd]
        gn = gn_ref[:, sl]
        state = state_ref[hd]
        for c in range(rows // CHUNK):
            r = slice(c * CHUNK, (c + 1) * CHUNK)
            q, k, v = rq_ref[r, sl], rk_ref[r, sl], rv_ref[r, sl]
            s = _dot_nt(q, k) * dmask
            o = _dot(s.astype(bf16), v) + _dot(q, state.astype(bf16)) * xi
            state = state * g_chunk + _dot_tn((k.astype(f32) * zeta).astype(bf16), v)
            g = rg_ref[r, sl]
            o_ref[r, sl] = (_rms(o, gn) * (g * jax.nn.sigmoid(g))).astype(bf16)
        state_ref[hd] = state


def _retention(log_g, rq, rk, rv, rg, gn_w, rk_meta, rv_meta, *, batch, seq, rows):
    steps = seq // rows
    blk = pl.BlockSpec((rows, D_RET), lambda b, t: (b * steps + t, 0))
    head_tab = pltpu.VMEM((RET_HEADS, CHUNK, CHUNK), f32)
    return pl.pallas_call(
        functools.partial(_retention_body, rows=rows),
        grid=(batch, steps),
        in_specs=[
            _resident((RET_HEADS, 1, LANES)),
            blk, blk, blk, blk,
            _resident((1, D_RET)),
            _resident((N_META, D_RET)),
            _resident((N_META, D_RET)),
        ],
        out_specs=blk,
        out_shape=jax.ShapeDtypeStruct((batch * seq, D_RET), bf16),
        scratch_shapes=[head_tab, head_tab, head_tab,
                        pltpu.VMEM((RET_HEADS, RET_DIM, RET_DIM), f32)],
        compiler_params=pltpu.CompilerParams(
            dimension_semantics=("parallel", "arbitrary"), vmem_limit_bytes=VMEM_LIMIT),
        name="retention",
    )(log_g, rq, rk, rv, rg, gn_w, rk_meta, rv_meta)


def _transpose_bf16(x):
    return x.astype(f32).T.astype(bf16)


def _mla_body(q_ref, k_ref, v_ref, km_ref, vm_ref, o_ref,
              qt_ref, vt_ref, s_ref, m_ref, l_ref, acc_ref, *, tq, seq, hg):
    qi = pl.program_id(2)
    heads = range(hg)

    def qk_cols(g):
        return slice(g * MLA_QK_PAD, (g + 1) * MLA_QK_PAD)

    def v_cols(g):
        return slice(g * MLA_V, (g + 1) * MLA_V)

    @pl.when(qi == 0)
    def _():
        for g in heads:
            for j in range(seq // tq):
                vt_ref[g, j] = _transpose_bf16(v_ref[j * tq:(j + 1) * tq, v_cols(g)])

    for g in heads:
        qt_ref[g] = _transpose_bf16(q_ref[:, qk_cols(g)])
        s = _dot(km_ref[:, qk_cols(g)], qt_ref[g])
        m0 = jnp.max(s, axis=0, keepdims=True)
        p = jnp.exp2(s - m0)
        m_ref[g] = m0
        l_ref[g] = jnp.sum(p, axis=0, keepdims=True)
        acc_ref[g] = _dot_tn(vm_ref[:, v_cols(g)], p.astype(bf16))

    def scores(j, slot):
        r = pl.ds(pl.multiple_of(j * tq, tq), tq)
        for g in heads:
            s_ref[slot, g] = _dot(k_ref[r, qk_cols(g)], qt_ref[g])

    def block(j, slot, masked):
        for g in heads:
            s = s_ref[slot, g]
            if masked:
                kpos = lax.broadcasted_iota(jnp.int32, (tq, tq), 0)
                qpos = lax.broadcasted_iota(jnp.int32, (tq, tq), 1)
                s = jnp.where(kpos <= qpos, s, -1e30)
            m_old = m_ref[g]
            m_new = jnp.maximum(m_old, jnp.max(s, axis=0, keepdims=True))
            alpha = jnp.exp2(m_old - m_new)
            p = jnp.exp2(s - m_new)
            l_ref[g] = alpha * l_ref[g] + jnp.sum(p, axis=0, keepdims=True)
            acc_ref[g] = alpha * acc_ref[g] + _dot(vt_ref[g, j], p.astype(bf16))
            m_ref[g] = m_new

    def step(j, slot):
        scores(j + 1, 1 - slot)
        block(j, slot, masked=False)

    def pair(i, carry):
        step(2 * i, 0)
        step(2 * i + 1, 1)
        return carry

    scores(0, 0)
    lax.fori_loop(0, lax.shift_right_logical(qi, 1), pair, 0)
    odd = (qi & 1) == 1

    @pl.when(odd)
    def _():
        step(qi - 1, 0)
        block(qi, 1, masked=True)

    @pl.when(jnp.logical_not(odd))
    def _():
        block(qi, 0, masked=True)

    for g in heads:
        o_ref[:, v_cols(g)] = (acc_ref[g] / l_ref[g]).T.astype(bf16)


def _mla(qc, kc, v, kc_meta, v_meta, *, batch, seq, tq, hg):
    nq = seq // tq
    return pl.pallas_call(
        functools.partial(_mla_body, tq=tq, seq=seq, hg=hg),
        grid=(batch, MLA_HEADS // hg, nq),
        in_specs=[
            pl.BlockSpec((tq, hg * MLA_QK_PAD), lambda b, h, i: (b * nq + i, h)),
            pl.BlockSpec((seq, hg * MLA_QK_PAD), lambda b, h, i: (b, h)),
            pl.BlockSpec((seq, hg * MLA_V), lambda b, h, i: (b, h)),
            pl.BlockSpec((N_META, hg * MLA_QK_PAD), lambda b, h, i: (0, h)),
            pl.BlockSpec((N_META, hg * MLA_V), lambda b, h, i: (0, h)),
        ],
        out_specs=pl.BlockSpec((tq, hg * MLA_V), lambda b, h, i: (b * nq + i, h)),
        out_shape=jax.ShapeDtypeStruct((batch * seq, D_MLA), bf16),
        scratch_shapes=[
            pltpu.VMEM((hg, MLA_QK_PAD, tq), bf16),
            pltpu.VMEM((hg, nq, MLA_V, tq), bf16),
            pltpu.VMEM((2, hg, tq, tq), f32),
            pltpu.VMEM((hg, 1, tq), f32),
            pltpu.VMEM((hg, 1, tq), f32),
            pltpu.VMEM((hg, MLA_V, tq), f32),
        ],
        compiler_params=pltpu.CompilerParams(
            dimension_semantics=("parallel", "parallel", "arbitrary"),
            vmem_limit_bytes=VMEM_LIMIT),
        name="mla_attention",
    )(qc, kc, v, kc_meta, v_meta)


def _mix_out_body(ret_ref, mla_ref, wa_ref, wb_ref, h_ref, post_ref, o_ref):
    m = _dot(ret_ref[...], wa_ref[...]) + _dot(mla_ref[...], wb_ref[...])
    o_ref[...] = h_ref[...] + _rms(m, post_ref[...])


def _mix_out(ret, mla, wo_ret, wo_mla, h, post_w, *, tm):
    n = h.shape[0]
    return pl.pallas_call(
        _mix_out_body,
        grid=(n // tm,),
        in_specs=[
            pl.BlockSpec((tm, D_RET), lambda i: (i, 0)),
            pl.BlockSpec((tm, D_MLA), lambda i: (i, 0)),
            _resident(wo_ret.shape),
            _resident(wo_mla.shape),
            pl.BlockSpec((tm, D_MODEL), lambda i: (i, 0)),
            _resident((1, D_MODEL)),
        ],
        out_specs=pl.BlockSpec((tm, D_MODEL), lambda i: (i, 0)),
        out_shape=jax.ShapeDtypeStruct((n, D_MODEL), f32),
        compiler_params=pltpu.CompilerParams(
            dimension_semantics=("parallel",), vmem_limit_bytes=VMEM_LIMIT),
        name="mix_out",
    )(ret, mla, wo_ret, wo_mla, h, post_w)


def _rope_tables(seq_len):
    pos = jnp.arange(seq_len, dtype=f32)

    def cos_sin(dim):
        inv = ROPE_THETA ** (-jnp.arange(0, dim, 2, dtype=f32) / dim)
        ang = pos[:, None] * inv[None, :]
        return jnp.cos(ang), jnp.sin(ang)

    c, s = cos_sin(RET_DIM)
    cr = jnp.concatenate([c, c], axis=-1)
    sr = jnp.concatenate([-s, s], axis=-1)
    c, s = cos_sin(MLA_ROPE)
    z = jnp.zeros_like(c)
    cm = jnp.concatenate([c, z, c, z], axis=-1)
    sm = jnp.concatenate([-s, z, s, z], axis=-1)
    return cr, sr, cm, sm


def _pad_rope_cols(w):
    half = MLA_ROPE // 2
    z = jnp.zeros(w.shape[:-1] + (half,), w.dtype)
    return jnp.concatenate([w[..., :half], z, w[..., half:], z], axis=-1)


def kernel(x, meta_tokens, ffn1_pre_norm, ffn1_w_gate, ffn1_w_up, ffn1_w_down, ffn1_post_norm, mix_pre_norm, w_in, ret_group_norm, mla_q_norm, mla_w_uq, mla_kv_norm, mla_w_uk, mla_w_uv, w_out, mix_post_norm, ffn2_pre_norm, ffn2_w_gate, ffn2_w_up, ffn2_w_down, ffn2_post_norm):
    batch, seq, _ = x.shape
    assert ffn1_pre_norm.shape[0] == 1, "single-layer trunk only"
    l = 0
    cr, sr, cm, sm = _rope_tables(N_META + seq)
    tabs_meta = tuple(t[:N_META] for t in (cr, sr, cm, sm))
    tabs_real = tuple(t[N_META:] for t in (cr, sr, cm, sm))
    log_g = jnp.log(1.0 - 2.0 ** (-5.0 - jnp.arange(RET_HEADS, dtype=f32)))
    log_g = jnp.broadcast_to(log_g[:, None, None], (RET_HEADS, 1, LANES))

    h = x.reshape(batch * seq, D_MODEL)
    hm = meta_tokens.astype(x.dtype)
    vec = lambda w: w[l].reshape(1, -1)
    ffn1 = (vec(ffn1_pre_norm), ffn1_w_gate[l].astype(bf16), ffn1_w_up[l].astype(bf16),
            ffn1_w_down[l].astype(bf16), vec(ffn1_post_norm))
    ffn2 = (vec(ffn2_pre_norm), ffn2_w_gate[l].astype(bf16), ffn2_w_up[l].astype(bf16),
            ffn2_w_down[l].astype(bf16), vec(ffn2_post_norm))
    win = jnp.concatenate(
        [w_in[l][:, :_OFF_KR], _pad_rope_cols(w_in[l][:, _OFF_KR:])], axis=-1).astype(bf16)
    wuq = mla_w_uq[l].reshape(MLA_Q_RANK, MLA_HEADS, MLA_NOPE + MLA_ROPE)
    wuq = jnp.concatenate([wuq[..., :MLA_NOPE], _pad_rope_cols(wuq[..., MLA_NOPE:])], axis=-1)
    wuq = wuq.reshape(MLA_Q_RANK, MLA_HEADS * MLA_QK_PAD).astype(bf16)
    mix_w = (vec(mix_pre_norm), win, vec(mla_q_norm), wuq, vec(mla_kv_norm),
             mla_w_uk[l].astype(bf16), mla_w_uv[l].astype(bf16))
    wo = w_out[l].astype(bf16)

    h = _ffn(h, *ffn1, tm=512, tf=512)
    hm = _ffn(hm, *ffn1, tm=N_META, tf=512)
    _, rk_m, rv_m, _, _, kc_m, v_m = _mix_in(hm, *mix_w, *tabs_meta, tm=N_META)
    rq, rk, rv, rg, qc, kc, v = _mix_in(h, *mix_w, *tabs_real, tm=256)
    ret = _retention(log_g, rq, rk, rv, rg, vec(ret_group_norm), rk_m, rv_m,
                     batch=batch, seq=seq, rows=512)
    mla = _mla(qc, kc, v, kc_m, v_m, batch=batch, seq=seq, tq=512, hg=2)
    h = _mix_out(ret, mla, wo[:D_RET], wo[D_RET:], h, vec(mix_post_norm), tm=512)
    h = _ffn(h, *ffn2, tm=512, tf=512)
    return h.reshape(batch, seq, D_MODEL)
```

```python
import functools

import jax
import jax.numpy as jnp
import numpy as np
from jax import lax
from jax.experimental import pallas as pl
from jax.experimental.pallas import tpu as pltpu

D_MODEL = 2048
N_META = 16
CHUNK = 128
RET_HEADS = 8
RET_DIM = 128
MLA_HEADS = 8
MLA_NOPE = 128
MLA_ROPE = 64
MLA_V = 128
MLA_Q_RANK = 512
MLA_KV_RANK = 256
MLA_QK_PAD = 256
D_RET = RET_HEADS * RET_DIM
D_MLA = MLA_HEADS * MLA_V
D_FF = 5632
ROPE_THETA = 10000.0
EPS = 1e-6
LANES = 128

_OFF_RQ = 0
_OFF_RK = D_RET
_OFF_RV = 2 * D_RET
_OFF_RG = 3 * D_RET
_OFF_CQ = 4 * D_RET
_OFF_CKV = _OFF_CQ + MLA_Q_RANK
_OFF_KR = _OFF_CKV + MLA_KV_RANK

MLA_Q_SCALE = (MLA_NOPE + MLA_ROPE) ** -0.5 * 1.4426950408889634

VMEM_LIMIT = 60 * 1024 * 1024

f32 = jnp.float32
bf16 = jnp.bfloat16


def _rms(x, w):
    return x * lax.rsqrt(jnp.mean(x * x, axis=-1, keepdims=True) + EPS) * w


def _dot(a, b):
    return jnp.dot(a, b, preferred_element_type=f32)


def _dot_nt(a, b):
    return lax.dot_general(a, b, (((1,), (1,)), ((), ())), preferred_element_type=f32)


def _dot_tn(a, b):
    return lax.dot_general(a, b, (((0,), (0,)), ((), ())), preferred_element_type=f32)


def _resident(shape):
    nd = len(shape)
    return pl.BlockSpec(shape, lambda *_: (0,) * nd, pipeline_mode=pl.Buffered(1))


def _ffn_body(*refs, ts, norm_input, emit_next):
    refs = list(refs)
    h_ref = refs.pop(0)
    pre_ref = refs.pop(0)
    wg_ref, wu_ref, wd_ref, post_ref = (refs.pop(0) for _ in range(4))
    npre_ref = refs.pop(0) if emit_next else None
    o_ref = refs.pop(0)
    xo_ref = refs.pop(0) if emit_next else None
    xn_ref = refs.pop(0) if norm_input else pre_ref
    j = pl.program_id(1)
    last = pl.num_programs(1) - 1

    def run(first, final):
        for r in range(h_ref.shape[0] // ts):
            rows = slice(r * ts, (r + 1) * ts)
            if norm_input and first:
                xn = _rms(h_ref[rows, :], pre_ref[...]).astype(bf16)
                xn_ref[rows, :] = xn
            else:
                xn = xn_ref[rows, :]
            g = _dot(xn, wg_ref[...])
            u = _dot(xn, wu_ref[...])
            a = (g * jax.nn.sigmoid(g) * u).astype(bf16)
            acc = _dot(a, wd_ref[...])
            if not first:
                acc += o_ref[rows, :]
            if final:
                acc = h_ref[rows, :] + 0.5 * _rms(acc, post_ref[...])
                if emit_next:
                    xo_ref[rows, :] = _rms(acc, npre_ref[...]).astype(bf16)
            o_ref[rows, :] = acc

    pl.when(j == 0)(lambda: run(True, False))
    pl.when(jnp.logical_and(j > 0, j < last))(lambda: run(False, False))
    pl.when(j == last)(lambda: run(False, True))


def _ffn(h, pre, wg, wu, wd, post_w, next_pre_w=None, *, tm, ts, tf):
    n = h.shape[0]
    norm_input = pre.shape[0] == 1
    emit_next = next_pre_w is not None
    assert D_FF // tf >= 2 and tm % ts == 0
    row = pl.BlockSpec((tm, D_MODEL), lambda i, j: (i, 0))
    vec = pl.BlockSpec((1, D_MODEL), lambda i, j: (0, 0))
    in_specs = [
        row,
        vec if norm_input else row,
        pl.BlockSpec((D_MODEL, tf), lambda i, j: (0, j)),
        pl.BlockSpec((D_MODEL, tf), lambda i, j: (0, j)),
        pl.BlockSpec((tf, D_MODEL), lambda i, j: (j, 0)),
        vec,
    ]
    args = [h, pre, wg, wu, wd, post_w]
    out_specs, out_shape = [row], [jax.ShapeDtypeStruct((n, D_MODEL), f32)]
    if emit_next:
        in_specs.append(vec)
        args.append(next_pre_w)
        out_specs.append(row)
        out_shape.append(jax.ShapeDtypeStruct((n, D_MODEL), bf16))
    out = pl.pallas_call(
        functools.partial(_ffn_body, ts=ts, norm_input=norm_input, emit_next=emit_next),
        grid=(n // tm, D_FF // tf),
        in_specs=in_specs,
        out_specs=out_specs,
        out_shape=out_shape,
        scratch_shapes=[pltpu.VMEM((tm, D_MODEL), bf16)] if norm_input else [],
        compiler_params=pltpu.CompilerParams(
            dimension_semantics=("parallel", "arbitrary"), vmem_limit_bytes=VMEM_LIMIT),
        name="swiglu",
    )(*args)
    return out if emit_next else out[0]


def _rot(x, c, s):
    return x * c + pltpu.roll(x, 64, 1) * s


def _mix_in_body(x_ref, win_ref, wkr_ref, qn_ref, wuq_ref, kvn_ref, wuk_ref, wuv_ref,
                 cr_ref, sr_ref, cm_ref, sm_ref,
                 rq_ref, rk_ref, rv_ref, rg_ref, qc_ref, kc_ref, v_ref):
    xn = x_ref[...]
    cr, sr = cr_ref[...], sr_ref[...]
    cm, sm = cm_ref[...], sm_ref[...]
    k_scale = RET_DIM ** -0.5
    pair = 2 * RET_DIM

    def proj(lo, width):
        return _dot(xn, win_ref[:, lo:lo + width])

    for lo in range(0, D_RET, pair):
        pq, pk = proj(_OFF_RQ + lo, pair), proj(_OFF_RK + lo, pair)
        for off in (0, RET_DIM):
            sl = slice(lo + off, lo + off + RET_DIM)
            rq_ref[:, sl] = _rot(pq[:, off:off + RET_DIM], cr, sr).astype(bf16)
            rk_ref[:, sl] = (_rot(pk[:, off:off + RET_DIM], cr, sr) * k_scale).astype(bf16)
    for lo in range(0, D_RET, pair):
        rv_ref[:, lo:lo + pair] = proj(_OFF_RV + lo, pair).astype(bf16)
        rg_ref[:, lo:lo + pair] = proj(_OFF_RG + lo, pair)

    cq = _rms(proj(_OFF_CQ, MLA_Q_RANK), qn_ref[...]).astype(bf16)
    ckv = _rms(proj(_OFF_CKV, MLA_KV_RANK), kvn_ref[...]).astype(bf16)
    kr = _rot(_dot(xn, wkr_ref[...]), cm, sm).astype(bf16)
    for hd in range(MLA_HEADS):
        lo = hd * MLA_QK_PAD
        q = _dot(cq, wuq_ref[:, lo:lo + MLA_QK_PAD])
        qc_ref[:, lo:lo + MLA_NOPE] = (q[:, :MLA_NOPE] * MLA_Q_SCALE).astype(bf16)
        qc_ref[:, lo + MLA_NOPE:lo + MLA_QK_PAD] = (
            _rot(q[:, MLA_NOPE:], cm, sm) * MLA_Q_SCALE).astype(bf16)
        kc_ref[:, lo + MLA_NOPE:lo + MLA_QK_PAD] = kr
    for lo in range(0, D_MLA, 2 * MLA_V):
        kn = _dot(ckv, wuk_ref[:, lo:lo + 2 * MLA_V])
        for off in (0, MLA_NOPE):
            hd = (lo + off) // MLA_NOPE
            kc_ref[:, hd * MLA_QK_PAD:hd * MLA_QK_PAD + MLA_NOPE] = (
                kn[:, off:off + MLA_NOPE].astype(bf16))
        v_ref[:, lo:lo + 2 * MLA_V] = _dot(ckv, wuv_ref[:, lo:lo + 2 * MLA_V]).astype(bf16)


def _mix_in(xn, win, wkr, qn_w, wuq, kvn_w, wuk, wuv, cr, sr, cm, sm, *, tm):
    n = xn.shape[0]
    n_pos_blocks = cr.shape[0] // tm

    def row(width):
        return pl.BlockSpec((tm, width), lambda i: (i, 0))

    tab = pl.BlockSpec((tm, LANES), lambda i: (i % n_pos_blocks, 0))
    d_qk = MLA_HEADS * MLA_QK_PAD
    out_shape = [
        jax.ShapeDtypeStruct((n, D_RET), bf16),
        jax.ShapeDtypeStruct((n, D_RET), bf16),
        jax.ShapeDtypeStruct((n, D_RET), bf16),
        jax.ShapeDtypeStruct((n, D_RET), f32),
        jax.ShapeDtypeStruct((n, d_qk), bf16),
        jax.ShapeDtypeStruct((n, d_qk), bf16),
        jax.ShapeDtypeStruct((n, D_MLA), bf16),
    ]
    return pl.pallas_call(
        _mix_in_body,
        grid=(n // tm,),
        in_specs=[
            row(D_MODEL),
            _resident(win.shape),
            _resident(wkr.shape),
            _resident((1, MLA_Q_RANK)),
            _resident(wuq.shape),
            _resident((1, MLA_KV_RANK)),
            _resident(wuk.shape),
            _resident(wuv.shape),
            tab, tab, tab, tab,
        ],
        out_specs=[row(D_RET), row(D_RET), row(D_RET), row(D_RET), row(d_qk), row(d_qk),
                   row(D_MLA)],
        out_shape=out_shape,
        compiler_params=pltpu.CompilerParams(
            dimension_semantics=("parallel",), vmem_limit_bytes=VMEM_LIMIT),
        name="mix_in",
    )(xn, win, wkr, qn_w, wuq, kvn_w, wuk, wuv, cr, sr, cm, sm)


def _retention_body(lg_ref, rq_ref, rk_ref, rv_ref, rg_ref, gn_ref, km_ref, vm_ref, o_ref,
                    dmask_ref, xi_ref, zeta_ref, state_ref, *, rows):
    t = pl.program_id(1)

    @pl.when(t == 0)
    def _():
        row = lax.broadcasted_iota(jnp.int32, (CHUNK, CHUNK), 0).astype(f32)
        col = lax.broadcasted_iota(jnp.int32, (CHUNK, CHUNK), 1).astype(f32)
        diff = row - col
        mrow = lax.broadcasted_iota(jnp.int32, (N_META, RET_DIM), 0).astype(f32)
        for hd in range(RET_HEADS):
            lg = lg_ref[hd]
            sl = slice(hd * RET_DIM, (hd + 1) * RET_DIM)
            dmask_ref[hd] = jnp.where(diff >= 0, jnp.exp(jnp.maximum(diff, 0.0) * lg), 0.0)
            xi_ref[hd] = jnp.exp((row + 1.0) * lg)
            zeta_ref[hd] = jnp.exp((CHUNK - 1.0 - row) * lg)
            zeta_m = jnp.exp((N_META - 1.0 - mrow) * lg)
            state_ref[hd] = _dot_tn((km_ref[:, sl].astype(f32) * zeta_m).astype(bf16),
                                    vm_ref[:, sl])

    for hd in range(RET_HEADS):
        sl = slice(hd * RET_DIM, (hd + 1) * RET_DIM)
        g_chunk = jnp.exp(CHUNK * lg_ref[hd])
        dmask, xi, zeta = dmask_ref[hd], xi_ref[hd], zeta_ref[hd]
        gn = gn_ref[:, sl]
        state = state_ref[hd]
        for c in range(rows // CHUNK):
            r = slice(c * CHUNK, (c + 1) * CHUNK)
            q, k, v = rq_ref[r, sl], rk_ref[r, sl], rv_ref[r, sl]
            s = _dot_nt(q, k) * dmask
            o = _dot(s.astype(bf16), v) + _dot(q, state.astype(bf16)) * xi
            state = state * g_chunk + _dot_tn((k.astype(f32) * zeta).astype(bf16), v)
            g = rg_ref[r, sl]
            o_ref[r, sl] = (_rms(o, gn) * (g * jax.nn.sigmoid(g))).astype(bf16)
        state_ref[hd] = state


def _retention(log_g, rq, rk, rv, rg, gn_w, rk_meta, rv_meta, *, batch, seq, rows):
    steps = seq // rows
    blk = pl.BlockSpec((rows, D_RET), lambda b, t: (b * steps + t, 0))
    head_tab = pltpu.VMEM((RET_HEADS, CHUNK, CHUNK), f32)
    return pl.pallas_call(
        functools.partial(_retention_body, rows=rows),
        grid=(batch, steps),
        in_specs=[
            _resident((RET_HEADS, 1, LANES)),
            blk, blk, blk, blk,
            _resident((1, D_RET)),
            _resident((N_META, D_RET)),
            _resident((N_META, D_RET)),
        ],
        out_specs=blk,
        out_shape=jax.ShapeDtypeStruct((batch * seq, D_RET), bf16),
        scratch_shapes=[head_tab, head_tab, head_tab,
                        pltpu.VMEM((RET_HEADS, RET_DIM, RET_DIM), f32)],
        compiler_params=pltpu.CompilerParams(
            dimension_semantics=("parallel", "arbitrary"), vmem_limit_bytes=VMEM_LIMIT),
        name="retention",
    )(log_g, rq, rk, rv, rg, gn_w, rk_meta, rv_meta)


def _transpose_bf16(x):
    return x.astype(f32).T.astype(bf16)


def _mla_body(q_ref, k_ref, v_ref, km_ref, vm_ref, o_ref,
              qt_ref, vt_ref, s_ref, m_ref, l_ref, acc_ref, *, tq, seq, hg):
    qi = pl.program_id(2)
    heads = range(hg)

    def qk_cols(g):
        return slice(g * MLA_QK_PAD, (g + 1) * MLA_QK_PAD)

    def v_cols(g):
        return slice(g * MLA_V, (g + 1) * MLA_V)

    @pl.when(qi == 0)
    def _():
        for g in heads:
            for j in range(seq // tq):
                vt_ref[g, j] = _transpose_bf16(v_ref[j * tq:(j + 1) * tq, v_cols(g)])

    for g in heads:
        qt_ref[g] = _transpose_bf16(q_ref[:, qk_cols(g)])

    def meta_keys():
        for g in heads:
            s = _dot(km_ref[:, qk_cols(g)], qt_ref[g])
            m0 = jnp.max(s, axis=0, keepdims=True)
            p = jnp.exp2(s - m0)
            m_ref[g] = m0
            l_ref[g] = jnp.sum(p, axis=0, keepdims=True)
            acc_ref[g] = _dot_tn(vm_ref[:, v_cols(g)], p.astype(bf16))

    def scores(j, slot):
        r = pl.ds(pl.multiple_of(j * tq, tq), tq)
        for g in heads:
            s_ref[slot, g] = _dot(k_ref[r, qk_cols(g)], qt_ref[g])

    def block(j, slot, masked):
        for g in heads:
            s = s_ref[slot, g]
            if masked:
                kpos = lax.broadcasted_iota(jnp.int32, (tq, tq), 0)
                qpos = lax.broadcasted_iota(jnp.int32, (tq, tq), 1)
                s = jnp.where(kpos <= qpos, s, -1e30)
            m_old = m_ref[g]
            m_new = jnp.maximum(m_old, jnp.max(s, axis=0, keepdims=True))
            alpha = jnp.exp2(m_old - m_new)
            p = jnp.exp2(s - m_new)
            l_ref[g] = alpha * l_ref[g] + jnp.sum(p, axis=0, keepdims=True)
            acc_ref[g] = alpha * acc_ref[g] + _dot(vt_ref[g, j], p.astype(bf16))
            m_ref[g] = m_new

    def step(j, slot):
        scores(j + 1, 1 - slot)
        block(j, slot, masked=False)

    def pair(i, carry):
        step(2 * i, 0)
        step(2 * i + 1, 1)
        return carry

    scores(0, 0)
    meta_keys()
    lax.fori_loop(0, lax.shift_right_logical(qi, 1), pair, 0)
    odd = (qi & 1) == 1

    @pl.when(odd)
    def _():
        step(qi - 1, 0)
        block(qi, 1, masked=True)

    @pl.when(jnp.logical_not(odd))
    def _():
        block(qi, 0, masked=True)

    for g in heads:
        o_ref[:, v_cols(g)] = (acc_ref[g] / l_ref[g]).T.astype(bf16)


def _mla(qc, kc, v, kc_meta, v_meta, *, batch, seq, tq, hg):
    nq = seq // tq
    return pl.pallas_call(
        functools.partial(_mla_body, tq=tq, seq=seq, hg=hg),
        grid=(batch, MLA_HEADS // hg, nq),
        in_specs=[
            pl.BlockSpec((tq, hg * MLA_QK_PAD), lambda b, h, i: (b * nq + i, h)),
            pl.BlockSpec((seq, hg * MLA_QK_PAD), lambda b, h, i: (b, h)),
            pl.BlockSpec((seq, hg * MLA_V), lambda b, h, i: (b, h)),
            pl.BlockSpec((N_META, hg * MLA_QK_PAD), lambda b, h, i: (0, h)),
            pl.BlockSpec((N_META, hg * MLA_V), lambda b, h, i: (0, h)),
        ],
        out_specs=pl.BlockSpec((tq, hg * MLA_V), lambda b, h, i: (b * nq + i, h)),
        out_shape=jax.ShapeDtypeStruct((batch * seq, D_MLA), bf16),
        scratch_shapes=[
            pltpu.VMEM((hg, MLA_QK_PAD, tq), bf16),
            pltpu.VMEM((hg, nq, MLA_V, tq), bf16),
            pltpu.VMEM((2, hg, tq, tq), f32),
            pltpu.VMEM((hg, 1, tq), f32),
            pltpu.VMEM((hg, 1, tq), f32),
            pltpu.VMEM((hg, MLA_V, tq), f32),
        ],
        compiler_params=pltpu.CompilerParams(
            dimension_semantics=("parallel", "parallel", "arbitrary"),
            vmem_limit_bytes=VMEM_LIMIT),
        name="mla_attention",
    )(qc, kc, v, kc_meta, v_meta)


def _mix_out_body(ret_ref, mla_ref, wo_ref, h_ref, post_ref, npre_ref, o_ref, xo_ref, *, ts):
    for r in range(h_ref.shape[0] // ts):
        rows = slice(r * ts, (r + 1) * ts)
        m = (_dot(ret_ref[rows, :], wo_ref[:D_RET, :]) +
             _dot(mla_ref[rows, :], wo_ref[D_RET:, :]))
        hn = h_ref[rows, :] + _rms(m, post_ref[...])
        o_ref[rows, :] = hn
        xo_ref[rows, :] = _rms(hn, npre_ref[...]).astype(bf16)


def _mix_out(ret, mla, wo, h, post_w, next_pre_w, *, tm, ts):
    n = h.shape[0]
    row = pl.BlockSpec((tm, D_MODEL), lambda i: (i, 0))
    return pl.pallas_call(
        functools.partial(_mix_out_body, ts=ts),
        grid=(n // tm,),
        in_specs=[
            pl.BlockSpec((tm, D_RET), lambda i: (i, 0)),
            pl.BlockSpec((tm, D_MLA), lambda i: (i, 0)),
            _resident(wo.shape),
            row,
            _resident((1, D_MODEL)),
            _resident((1, D_MODEL)),
        ],
        out_specs=[row, row],
        out_shape=[jax.ShapeDtypeStruct((n, D_MODEL), f32),
                   jax.ShapeDtypeStruct((n, D_MODEL), bf16)],
        compiler_params=pltpu.CompilerParams(
            dimension_semantics=("parallel",), vmem_limit_bytes=VMEM_LIMIT),
        name="mix_out",
    )(ret, mla, wo, h, post_w, next_pre_w)


def _rope_tables(seq_len):
    pos = np.arange(seq_len, dtype=np.float64)

    def cos_sin(dim):
        inv = ROPE_THETA ** (-np.arange(0, dim, 2, dtype=np.float64) / dim)
        ang = pos[:, None] * inv[None, :]
        return np.cos(ang), np.sin(ang)

    c, s = cos_sin(RET_DIM)
    cr = np.concatenate([c, c], axis=-1)
    sr = np.concatenate([-s, s], axis=-1)
    c, s = cos_sin(MLA_ROPE)
    z = np.zeros_like(c)
    cm = np.concatenate([c, z, c, z], axis=-1)
    sm = np.concatenate([-s, z, s, z], axis=-1)
    return tuple(t.astype(np.float32) for t in (cr, sr, cm, sm))


def _pad_rope_cols(w):
    half = MLA_ROPE // 2
    z = jnp.zeros(w.shape[:-1] + (half,), w.dtype)
    return jnp.concatenate([w[..., :half], z, w[..., half:], z], axis=-1)


def kernel(x, meta_tokens, ffn1_pre_norm, ffn1_w_gate, ffn1_w_up, ffn1_w_down, ffn1_post_norm, mix_pre_norm, w_in, ret_group_norm, mla_q_norm, mla_w_uq, mla_kv_norm, mla_w_uk, mla_w_uv, w_out, mix_post_norm, ffn2_pre_norm, ffn2_w_gate, ffn2_w_up, ffn2_w_down, ffn2_post_norm):
    batch, seq, _ = x.shape
    assert ffn1_pre_norm.shape[0] == 1, "single-layer trunk only"
    l = 0
    cr, sr, cm, sm = _rope_tables(N_META + seq)
    tabs_meta = tuple(t[:N_META] for t in (cr, sr, cm, sm))
    tabs_real = tuple(t[N_META:] for t in (cr, sr, cm, sm))
    log_g = jnp.log(1.0 - 2.0 ** (-5.0 - jnp.arange(RET_HEADS, dtype=f32)))
    log_g = jnp.broadcast_to(log_g[:, None, None], (RET_HEADS, 1, LANES))

    h = x.reshape(batch * seq, D_MODEL)
    hm = meta_tokens.astype(x.dtype)
    vec = lambda w: w[l].reshape(1, -1)
    ffn1 = (vec(ffn1_pre_norm), ffn1_w_gate[l].astype(bf16), ffn1_w_up[l].astype(bf16),
            ffn1_w_down[l].astype(bf16), vec(ffn1_post_norm))
    ffn2 = (vec(ffn2_pre_norm), ffn2_w_gate[l].astype(bf16), ffn2_w_up[l].astype(bf16),
            ffn2_w_down[l].astype(bf16), vec(ffn2_post_norm))
    win = w_in[l].astype(bf16)
    wkr = _pad_rope_cols(w_in[l][:, _OFF_KR:]).astype(bf16)
    wuq = mla_w_uq[l].reshape(MLA_Q_RANK, MLA_HEADS, MLA_NOPE + MLA_ROPE)
    wuq = jnp.concatenate([wuq[..., :MLA_NOPE], _pad_rope_cols(wuq[..., MLA_NOPE:])], axis=-1)
    wuq = wuq.reshape(MLA_Q_RANK, MLA_HEADS * MLA_QK_PAD).astype(bf16)
    mix_w = (win, wkr, vec(mla_q_norm), wuq, vec(mla_kv_norm),
             mla_w_uk[l].astype(bf16), mla_w_uv[l].astype(bf16))
    wo = w_out[l].astype(bf16)

    h, xn = _ffn(h, *ffn1, vec(mix_pre_norm), tm=1024, ts=256, tf=512)
    _, xn_m = _ffn(hm, *ffn1, vec(mix_pre_norm), tm=N_META, ts=N_META, tf=512)
    _, rk_m, rv_m, _, _, kc_m, v_m = _mix_in(xn_m, *mix_w, *tabs_meta, tm=N_META)
    rq, rk, rv, rg, qc, kc, v = _mix_in(xn, *mix_w, *tabs_real, tm=512)
    ret = _retention(log_g, rq, rk, rv, rg, vec(ret_group_norm), rk_m, rv_m,
                     batch=batch, seq=seq, rows=512)
    mla = _mla(qc, kc, v, kc_m, v_m, batch=batch, seq=seq, tq=512, hg=2)
    h, xn = _mix_out(ret, mla, wo, h, vec(mix_post_norm), ffn2[0], tm=512, ts=128)
    h = _ffn(h, xn, *ffn2[1:], tm=1024, ts=256, tf=512)
    return h.reshape(batch, seq, D_MODEL)
```

```python
import functools

import jax
import jax.numpy as jnp
import numpy as np
from jax import lax
from jax.experimental import pallas as pl
from jax.experimental.pallas import tpu as pltpu

D_MODEL = 2048
N_META = 16
CHUNK = 128
RET_HEADS = 8
RET_DIM = 128
MLA_HEADS = 8
MLA_NOPE = 128
MLA_ROPE = 64
MLA_V = 128
MLA_Q_RANK = 512
MLA_KV_RANK = 256
MLA_QK_PAD = 256
D_RET = RET_HEADS * RET_DIM
D_MLA = MLA_HEADS * MLA_V
D_FF = 5632
ROPE_THETA = 10000.0
EPS = 1e-6
LANES = 128

_OFF_RQ = 0
_OFF_RK = D_RET
_OFF_RV = 2 * D_RET
_OFF_RG = 3 * D_RET
_OFF_CQ = 4 * D_RET
_OFF_CKV = _OFF_CQ + MLA_Q_RANK
_OFF_KR = _OFF_CKV + MLA_KV_RANK

MLA_Q_SCALE = (MLA_NOPE + MLA_ROPE) ** -0.5 * 1.4426950408889634

VMEM_LIMIT = 60 * 1024 * 1024

f32 = jnp.float32
bf16 = jnp.bfloat16


def _rms(x, w):
    return x * lax.rsqrt(jnp.mean(x * x, axis=-1, keepdims=True) + EPS) * w


def _dot(a, b):
    return jnp.dot(a, b, preferred_element_type=f32)


def _dot_nt(a, b):
    return lax.dot_general(a, b, (((1,), (1,)), ((), ())), preferred_element_type=f32)


def _dot_tn(a, b):
    return lax.dot_general(a, b, (((0,), (0,)), ((), ())), preferred_element_type=f32)


def _resident(shape):
    nd = len(shape)
    return pl.BlockSpec(shape, lambda *_: (0,) * nd, pipeline_mode=pl.Buffered(1))


def _ffn_body(*refs, ts, norm_input, emit_next):
    refs = list(refs)
    h_ref = refs.pop(0)
    pre_ref = refs.pop(0)
    wg_ref, wu_ref, wd_ref, post_ref = (refs.pop(0) for _ in range(4))
    npre_ref = refs.pop(0) if emit_next else None
    o_ref = refs.pop(0)
    xo_ref = refs.pop(0) if emit_next else None
    xn_ref = refs.pop(0) if norm_input else pre_ref
    j = pl.program_id(1)
    last = pl.num_programs(1) - 1

    def run(first, final):
        step_rows = ts if (first or final) else min(2 * ts, h_ref.shape[0])
        for r in range(h_ref.shape[0] // step_rows):
            rows = slice(r * step_rows, (r + 1) * step_rows)
            if norm_input and first:
                xn = _rms(h_ref[rows, :], pre_ref[...]).astype(bf16)
                xn_ref[rows, :] = xn
            else:
                xn = xn_ref[rows, :]
            g = _dot(xn, wg_ref[...])
            u = _dot(xn, wu_ref[...])
            a = (g * jax.nn.sigmoid(g) * u).astype(bf16)
            acc = _dot(a, wd_ref[...])
            if not first:
                acc += o_ref[rows, :]
            if final:
                acc = h_ref[rows, :] + 0.5 * _rms(acc, post_ref[...])
                if emit_next:
                    xo_ref[rows, :] = _rms(acc, npre_ref[...]).astype(bf16)
            o_ref[rows, :] = acc

    pl.when(j == 0)(lambda: run(True, False))
    pl.when(jnp.logical_and(j > 0, j < last))(lambda: run(False, False))
    pl.when(j == last)(lambda: run(False, True))


def _ffn(h, pre, wg, wu, wd, post_w, next_pre_w=None, *, tm, ts, tf):
    n = h.shape[0]
    norm_input = pre.shape[0] == 1
    emit_next = next_pre_w is not None
    assert D_FF // tf >= 2 and tm % ts == 0
    row = pl.BlockSpec((tm, D_MODEL), lambda i, j: (i, 0))
    vec = pl.BlockSpec((1, D_MODEL), lambda i, j: (0, 0))
    in_specs = [
        row,
        vec if norm_input else row,
        pl.BlockSpec((D_MODEL, tf), lambda i, j: (0, j)),
        pl.BlockSpec((D_MODEL, tf), lambda i, j: (0, j)),
        pl.BlockSpec((tf, D_MODEL), lambda i, j: (j, 0)),
        vec,
    ]
    args = [h, pre, wg, wu, wd, post_w]
    out_specs, out_shape = [row], [jax.ShapeDtypeStruct((n, D_MODEL), f32)]
    if emit_next:
        in_specs.append(vec)
        args.append(next_pre_w)
        out_specs.append(row)
        out_shape.append(jax.ShapeDtypeStruct((n, D_MODEL), bf16))
    out = pl.pallas_call(
        functools.partial(_ffn_body, ts=ts, norm_input=norm_input, emit_next=emit_next),
        grid=(n // tm, D_FF // tf),
        in_specs=in_specs,
        out_specs=out_specs,
        out_shape=out_shape,
        scratch_shapes=[pltpu.VMEM((tm, D_MODEL), bf16)] if norm_input else [],
        compiler_params=pltpu.CompilerParams(
            dimension_semantics=("parallel", "arbitrary"), vmem_limit_bytes=VMEM_LIMIT),
        name="swiglu",
    )(*args)
    return out if emit_next else out[0]


def _rot(x, c, s):
    return x * c + pltpu.roll(x, 64, 1) * s


def _mix_in_body(x_ref, win_ref, wkr_ref, qn_ref, wuqt_ref, kvn_ref, wuk_ref, wuvt_ref,
                 cr_ref, sr_ref, cm_ref, sm_ref, cmt_ref, smt_ref,
                 rq_ref, rk_ref, rv_ref, rg_ref, qt_ref, kc_ref, vt_ref, *, ts):
    k_scale = RET_DIM ** -0.5
    pair = 2 * RET_DIM
    half = LANES // 2
    for r in range(x_ref.shape[0] // ts):
        rows = slice(r * ts, (r + 1) * ts)
        xn = x_ref[rows, :]
        cr, sr = cr_ref[rows, :], sr_ref[rows, :]
        cm, sm = cm_ref[rows, :], sm_ref[rows, :]

        def proj(lo, width):
            return _dot(xn, win_ref[:, lo:lo + width])

        for lo in range(0, D_RET, pair):
            pq, pk = proj(_OFF_RQ + lo, pair), proj(_OFF_RK + lo, pair)
            for off in (0, RET_DIM):
                sl = slice(lo + off, lo + off + RET_DIM)
                rq_ref[rows, sl] = _rot(pq[:, off:off + RET_DIM], cr, sr).astype(bf16)
                rk_ref[rows, sl] = (
                    _rot(pk[:, off:off + RET_DIM], cr, sr) * k_scale).astype(bf16)
        for lo in range(0, D_RET, pair):
            rv_ref[rows, lo:lo + pair] = proj(_OFF_RV + lo, pair).astype(bf16)
            rg_ref[rows, lo:lo + pair] = proj(_OFF_RG + lo, pair)

        cq = _rms(proj(_OFF_CQ, MLA_Q_RANK), qn_ref[...]).astype(bf16)
        ckv = _rms(proj(_OFF_CKV, MLA_KV_RANK), kvn_ref[...]).astype(bf16)
        kr = _rot(_dot(xn, wkr_ref[...]), cm, sm).astype(bf16)
        cmt, smt = cmt_ref[:, rows], smt_ref[:, rows]
        for hd in range(MLA_HEADS):
            lo = hd * MLA_QK_PAD
            qt = _dot_nt(wuqt_ref[lo:lo + MLA_QK_PAD, :], cq)
            qr = qt[MLA_NOPE:]
            qr = qr * cmt + jnp.concatenate([qr[half:], qr[:half]], axis=0) * smt
            qt_ref[lo:lo + MLA_NOPE, rows] = (qt[:MLA_NOPE] * MLA_Q_SCALE).astype(bf16)
            qt_ref[lo + MLA_NOPE:lo + MLA_QK_PAD, rows] = (qr * MLA_Q_SCALE).astype(bf16)
            kc_ref[rows, lo + MLA_NOPE:lo + MLA_QK_PAD] = kr
        for lo in range(0, D_MLA, 2 * MLA_V):
            kn = _dot(ckv, wuk_ref[:, lo:lo + 2 * MLA_V])
            for off in (0, MLA_NOPE):
                hd = (lo + off) // MLA_NOPE
                kc_ref[rows, hd * MLA_QK_PAD:hd * MLA_QK_PAD + MLA_NOPE] = (
                    kn[:, off:off + MLA_NOPE].astype(bf16))
            vt_ref[0, lo:lo + 2 * MLA_V, rows] = _dot_nt(wuvt_ref[lo:lo + 2 * MLA_V, :],
                                                         ckv).astype(bf16)


def _mix_in(xn, win, wkr, qn_w, wuqt, kvn_w, wuk, wuvt, cr, sr, cm, sm, cmt, smt, *, tm, ts):
    n = xn.shape[0]
    n_pos_blocks = cr.shape[0] // tm

    def row(width):
        return pl.BlockSpec((tm, width), lambda i: (i, 0))

    tab = pl.BlockSpec((tm, LANES), lambda i: (i % n_pos_blocks, 0))
    tab_t = pl.BlockSpec((LANES, tm), lambda i: (0, i % n_pos_blocks))
    d_qk = MLA_HEADS * MLA_QK_PAD
    out_shape = [
        jax.ShapeDtypeStruct((n, D_RET), bf16),
        jax.ShapeDtypeStruct((n, D_RET), bf16),
        jax.ShapeDtypeStruct((n, D_RET), bf16),
        jax.ShapeDtypeStruct((n, D_RET), f32),
        jax.ShapeDtypeStruct((d_qk, n), bf16),
        jax.ShapeDtypeStruct((n, d_qk), bf16),
        jax.ShapeDtypeStruct((n // tm, D_MLA, tm), bf16),
    ]
    return pl.pallas_call(
        functools.partial(_mix_in_body, ts=ts),
        grid=(n // tm,),
        in_specs=[
            row(D_MODEL),
            _resident(win.shape),
            _resident(wkr.shape),
            _resident((1, MLA_Q_RANK)),
            _resident(wuqt.shape),
            _resident((1, MLA_KV_RANK)),
            _resident(wuk.shape),
            _resident(wuvt.shape),
            tab, tab, tab, tab, tab_t, tab_t,
        ],
        out_specs=[row(D_RET), row(D_RET), row(D_RET), row(D_RET),
                   pl.BlockSpec((d_qk, tm), lambda i: (0, i)),
                   row(d_qk),
                   pl.BlockSpec((1, D_MLA, tm), lambda i: (i, 0, 0))],
        out_shape=out_shape,
        compiler_params=pltpu.CompilerParams(
            dimension_semantics=("parallel",), vmem_limit_bytes=VMEM_LIMIT),
        name="mix_in",
    )(xn, win, wkr, qn_w, wuqt, kvn_w, wuk, wuvt, cr, sr, cm, sm, cmt, smt)


def _retention_body(lg_ref, rq_ref, rk_ref, rv_ref, rg_ref, gn_ref, km_ref, vm_ref, o_ref,
                    dmask_ref, xi_ref, zeta_ref, state_ref, *, rows):
    t = pl.program_id(1)

    @pl.when(t == 0)
    def _():
        row = lax.broadcasted_iota(jnp.int32, (CHUNK, CHUNK), 0).astype(f32)
        col = lax.broadcasted_iota(jnp.int32, (CHUNK, CHUNK), 1).astype(f32)
        diff = row - col
        mrow = lax.broadcasted_iota(jnp.int32, (N_META, RET_DIM), 0).astype(f32)
        for hd in range(RET_HEADS):
            lg = lg_ref[hd]
            sl = slice(hd * RET_DIM, (hd + 1) * RET_DIM)
            dmask_ref[hd] = jnp.where(diff >= 0, jnp.exp(jnp.maximum(diff, 0.0) * lg), 0.0)
            xi_ref[hd] = jnp.exp((row + 1.0) * lg)
            zeta_ref[hd] = jnp.exp((CHUNK - 1.0 - row) * lg)
            zeta_m = jnp.exp((N_META - 1.0 - mrow) * lg)
            state_ref[hd] = _dot_tn((km_ref[:, sl].astype(f32) * zeta_m).astype(bf16),
                                    vm_ref[:, sl])

    chunks = [slice(c * CHUNK, (c + 1) * CHUNK) for c in range(rows // CHUNK)]

    def head_cols(hd):
        return slice(hd * RET_DIM, (hd + 1) * RET_DIM)

    def state_free(hd):
        sl = head_cols(hd)
        dmask, zeta = dmask_ref[hd], zeta_ref[hd]
        out = []
        for r in chunks:
            q, k, v = rq_ref[r, sl], rk_ref[r, sl], rv_ref[r, sl]
            s = (_dot_nt(q, k) * dmask).astype(bf16)
            inc = _dot_tn((k.astype(f32) * zeta).astype(bf16), v)
            out.append((s, inc))
        return out

    def state_bound(hd, staged):
        sl = head_cols(hd)
        g_chunk = jnp.exp(CHUNK * lg_ref[hd])
        xi, gn = xi_ref[hd], gn_ref[:, sl]
        state = state_ref[hd]
        for r, (s, inc) in zip(chunks, staged):
            o = _dot(s, rv_ref[r, sl]) + _dot(rq_ref[r, sl], state.astype(bf16)) * xi
            state = state * g_chunk + inc
            g = rg_ref[r, sl]
            o_ref[r, sl] = (_rms(o, gn) * (g * jax.nn.sigmoid(g))).astype(bf16)
        state_ref[hd] = state

    staged = state_free(0)
    for hd in range(RET_HEADS):
        ahead = state_free(hd + 1) if hd + 1 < RET_HEADS else None
        state_bound(hd, staged)
        staged = ahead


def _retention(log_g, rq, rk, rv, rg, gn_w, rk_meta, rv_meta, *, batch, seq, rows):
    steps = seq // rows
    blk = pl.BlockSpec((rows, D_RET), lambda b, t: (b * steps + t, 0))
    head_tab = pltpu.VMEM((RET_HEADS, CHUNK, CHUNK), f32)
    return pl.pallas_call(
        functools.partial(_retention_body, rows=rows),
        grid=(batch, steps),
        in_specs=[
            _resident((RET_HEADS, 1, LANES)),
            blk, blk, blk, blk,
            _resident((1, D_RET)),
            _resident((N_META, D_RET)),
            _resident((N_META, D_RET)),
        ],
        out_specs=blk,
        out_shape=jax.ShapeDtypeStruct((batch * seq, D_RET), bf16),
        scratch_shapes=[head_tab, head_tab, head_tab,
                        pltpu.VMEM((RET_HEADS, RET_DIM, RET_DIM), f32)],
        compiler_params=pltpu.CompilerParams(
            dimension_semantics=("parallel", "arbitrary"), vmem_limit_bytes=VMEM_LIMIT),
        name="retention",
    )(log_g, rq, rk, rv, rg, gn_w, rk_meta, rv_meta)


def _mla_body(qt_ref, k_ref, vt_ref, km_ref, vmt_ref, o_ref,
              s_ref, m_ref, l_ref, acc_ref, *, tq, hg):
    qi = pl.program_id(2)
    heads = range(hg)

    def qk_cols(g):
        return slice(g * MLA_QK_PAD, (g + 1) * MLA_QK_PAD)

    def v_cols(g):
        return slice(g * MLA_V, (g + 1) * MLA_V)

    def meta_keys():
        s_all = [_dot(km_ref[:, qk_cols(g)], qt_ref[qk_cols(g), :]) for g in heads]
        for g in heads:
            s = s_all[g]
            m0 = jnp.max(s, axis=0, keepdims=True)
            p = jnp.exp2(s - m0)
            m_ref[g] = m0
            l_ref[g] = jnp.sum(p, axis=0, keepdims=True)
            acc_ref[g] = _dot(vmt_ref[v_cols(g), :], p.astype(bf16))

    def scores(j, slot):
        r = pl.ds(pl.multiple_of(j * tq, tq), tq)
        for g in heads:
            s_ref[slot, g] = _dot(k_ref[r, qk_cols(g)], qt_ref[qk_cols(g), :])

    def block(j, slot, masked):
        for g in heads:
            s = s_ref[slot, g]
            if masked:
                kpos = lax.broadcasted_iota(jnp.int32, (tq, tq), 0)
                qpos = lax.broadcasted_iota(jnp.int32, (tq, tq), 1)
                s = jnp.where(kpos <= qpos, s, -1e30)
            m_old = m_ref[g]
            m_new = jnp.maximum(m_old, jnp.max(s, axis=0, keepdims=True))
            alpha = jnp.exp2(m_old - m_new)
            p = jnp.exp2(s - m_new)
            l_ref[g] = alpha * l_ref[g] + jnp.sum(p, axis=0, keepdims=True)
            acc_ref[g] = alpha * acc_ref[g] + _dot(vt_ref[j, v_cols(g), :], p.astype(bf16))
            m_ref[g] = m_new

    def step(j, slot):
        scores(j + 1, 1 - slot)
        block(j, slot, masked=False)

    def pair(i, carry):
        step(2 * i, 0)
        step(2 * i + 1, 1)
        return carry

    scores(0, 0)
    meta_keys()
    lax.fori_loop(0, lax.shift_right_logical(qi, 1), pair, 0)
    odd = (qi & 1) == 1

    @pl.when(odd)
    def _():
        step(qi - 1, 0)
        block(qi, 1, masked=True)

    @pl.when(jnp.logical_not(odd))
    def _():
        block(qi, 0, masked=True)

    for g in heads:
        o_ref[:, v_cols(g)] = (acc_ref[g] / l_ref[g]).T.astype(bf16)


def _mla(qt, kc, vt, kc_meta, vt_meta, *, batch, seq, hg):
    tq = vt.shape[-1]
    nq = seq // tq
    return pl.pallas_call(
        functools.partial(_mla_body, tq=tq, hg=hg),
        grid=(batch, MLA_HEADS // hg, nq),
        in_specs=[
            pl.BlockSpec((hg * MLA_QK_PAD, tq), lambda b, h, i: (h, b * nq + i)),
            pl.BlockSpec((seq, hg * MLA_QK_PAD), lambda b, h, i: (b, h)),
            pl.BlockSpec((nq, hg * MLA_V, tq), lambda b, h, i: (b, h, 0)),
            pl.BlockSpec((N_META, hg * MLA_QK_PAD), lambda b, h, i: (0, h)),
            pl.BlockSpec((hg * MLA_V, N_META), lambda b, h, i: (h, 0)),
        ],
        out_specs=pl.BlockSpec((tq, hg * MLA_V), lambda b, h, i: (b * nq + i, h)),
        out_shape=jax.ShapeDtypeStruct((batch * seq, D_MLA), bf16),
        scratch_shapes=[
            pltpu.VMEM((2, hg, tq, tq), f32),
            pltpu.VMEM((hg, 1, tq), f32),
            pltpu.VMEM((hg, 1, tq), f32),
            pltpu.VMEM((hg, MLA_V, tq), f32),
        ],
        compiler_params=pltpu.CompilerParams(
            dimension_semantics=("parallel", "parallel", "arbitrary"),
            vmem_limit_bytes=VMEM_LIMIT),
        name="mla_attention",
    )(qt, kc, vt, kc_meta, vt_meta)


def _mix_out_body(ret_ref, mla_ref, wo_ref, h_ref, post_ref, npre_ref, o_ref, xo_ref, *, ts):
    for r in range(h_ref.shape[0] // ts):
        rows = slice(r * ts, (r + 1) * ts)
        m = (_dot(ret_ref[rows, :], wo_ref[:D_RET, :]) +
             _dot(mla_ref[rows, :], wo_ref[D_RET:, :]))
        hn = h_ref[rows, :] + _rms(m, post_ref[...])
        o_ref[rows, :] = hn
        xo_ref[rows, :] = _rms(hn, npre_ref[...]).astype(bf16)


def _mix_out(ret, mla, wo, h, post_w, next_pre_w, *, tm, ts):
    n = h.shape[0]
    row = pl.BlockSpec((tm, D_MODEL), lambda i: (i, 0))
    return pl.pallas_call(
        functools.partial(_mix_out_body, ts=ts),
        grid=(n // tm,),
        in_specs=[
            pl.BlockSpec((tm, D_RET), lambda i: (i, 0)),
            pl.BlockSpec((tm, D_MLA), lambda i: (i, 0)),
            _resident(wo.shape),
            row,
            _resident((1, D_MODEL)),
            _resident((1, D_MODEL)),
        ],
        out_specs=[row, row],
        out_shape=[jax.ShapeDtypeStruct((n, D_MODEL), f32),
                   jax.ShapeDtypeStruct((n, D_MODEL), bf16)],
        compiler_params=pltpu.CompilerParams(
            dimension_semantics=("parallel",), vmem_limit_bytes=VMEM_LIMIT),
        name="mix_out",
    )(ret, mla, wo, h, post_w, next_pre_w)


def _rope_tables(seq_len):
    pos = np.arange(seq_len, dtype=np.float64)

    def cos_sin(dim):
        inv = ROPE_THETA ** (-np.arange(0, dim, 2, dtype=np.float64) / dim)
        ang = pos[:, None] * inv[None, :]
        return np.cos(ang), np.sin(ang)

    c, s = cos_sin(RET_DIM)
    cr = np.concatenate([c, c], axis=-1)
    sr = np.concatenate([-s, s], axis=-1)
    c, s = cos_sin(MLA_ROPE)
    z = np.zeros_like(c)
    cm = np.concatenate([c, z, c, z], axis=-1)
    sm = np.concatenate([-s, z, s, z], axis=-1)
    return tuple(t.astype(np.float32) for t in (cr, sr, cm, sm))


def _pad_rope_cols(w):
    half = MLA_ROPE // 2
    z = jnp.zeros(w.shape[:-1] + (half,), w.dtype)
    return jnp.concatenate([w[..., :half], z, w[..., half:], z], axis=-1)


def kernel(x, meta_tokens, ffn1_pre_norm, ffn1_w_gate, ffn1_w_up, ffn1_w_down, ffn1_post_norm, mix_pre_norm, w_in, ret_group_norm, mla_q_norm, mla_w_uq, mla_kv_norm, mla_w_uk, mla_w_uv, w_out, mix_post_norm, ffn2_pre_norm, ffn2_w_gate, ffn2_w_up, ffn2_w_down, ffn2_post_norm):
    batch, seq, _ = x.shape
    assert ffn1_pre_norm.shape[0] == 1, "single-layer trunk only"
    l = 0
    cr, sr, cm, sm = _rope_tables(N_META + seq)
    tabs_meta = tuple(t[:N_META] for t in (cr, sr, cm, sm)) + (cm[:N_META].T, sm[:N_META].T)
    tabs_real = tuple(t[N_META:] for t in (cr, sr, cm, sm)) + (cm[N_META:].T, sm[N_META:].T)
    log_g = jnp.log(1.0 - 2.0 ** (-5.0 - jnp.arange(RET_HEADS, dtype=f32)))
    log_g = jnp.broadcast_to(log_g[:, None, None], (RET_HEADS, 1, LANES))

    h = x.reshape(batch * seq, D_MODEL)
    hm = meta_tokens.astype(x.dtype)
    vec = lambda w: w[l].reshape(1, -1)
    ffn1 = (vec(ffn1_pre_norm), ffn1_w_gate[l].astype(bf16), ffn1_w_up[l].astype(bf16),
            ffn1_w_down[l].astype(bf16), vec(ffn1_post_norm))
    ffn2 = (vec(ffn2_pre_norm), ffn2_w_gate[l].astype(bf16), ffn2_w_up[l].astype(bf16),
            ffn2_w_down[l].astype(bf16), vec(ffn2_post_norm))
    win = w_in[l][:, :_OFF_KR].astype(bf16)
    wkr = _pad_rope_cols(w_in[l][:, _OFF_KR:]).astype(bf16)
    wuq = mla_w_uq[l].reshape(MLA_Q_RANK, MLA_HEADS, MLA_NOPE + MLA_ROPE)
    wuq = jnp.concatenate([wuq[..., :MLA_NOPE], _pad_rope_cols(wuq[..., MLA_NOPE:])], axis=-1)
    wuqt = wuq.reshape(MLA_Q_RANK, MLA_HEADS * MLA_QK_PAD).T.astype(bf16)
    mix_w = (win, wkr, vec(mla_q_norm), wuqt, vec(mla_kv_norm),
             mla_w_uk[l].astype(bf16), mla_w_uv[l].T.astype(bf16))
    wo = w_out[l].astype(bf16)

    h, xn = _ffn(h, *ffn1, vec(mix_pre_norm), tm=512, ts=256, tf=512)
    _, xn_m = _ffn(hm, *ffn1, vec(mix_pre_norm), tm=N_META, ts=N_META, tf=512)
    _, rk_m, rv_m, _, _, kc_m, vt_m = _mix_in(xn_m, *mix_w, *tabs_meta, tm=N_META, ts=N_META)
    rq, rk, rv, rg, qt, kc, vt = _mix_in(xn, *mix_w, *tabs_real, tm=512, ts=256)
    ret = _retention(log_g, rq, rk, rv, rg, vec(ret_group_norm), rk_m, rv_m,
                     batch=batch, seq=seq, rows=512)
    mla = _mla(qt, kc, vt, kc_m, vt_m[0], batch=batch, seq=seq, hg=4)
    h, xn = _mix_out(ret, mla, wo, h, vec(mix_post_norm), ffn2[0], tm=512, ts=128)
    h = _ffn(h, xn, *ffn2[1:], tm=1024, ts=256, tf=512)
    return h.reshape(batch, seq, D_MODEL)
```

```python
import functools

import jax
import jax.numpy as jnp
import numpy as np
from jax import lax
from jax.experimental import pallas as pl
from jax.experimental.pallas import tpu as pltpu

D_MODEL = 2048
N_META = 16
CHUNK = 128
RET_HEADS = 8
RET_DIM = 128
MLA_HEADS = 8
MLA_NOPE = 128
MLA_ROPE = 64
MLA_V = 128
MLA_Q_RANK = 512
MLA_KV_RANK = 256
MLA_QK_PAD = 256
D_RET = RET_HEADS * RET_DIM
D_MLA = MLA_HEADS * MLA_V
D_FF = 5632
ROPE_THETA = 10000.0
EPS = 1e-6
LANES = 128

_OFF_RQ = 0
_OFF_RK = D_RET
_OFF_RV = 2 * D_RET
_OFF_RG = 3 * D_RET
_OFF_CQ = 4 * D_RET
_OFF_CKV = _OFF_CQ + MLA_Q_RANK
_OFF_KR = _OFF_CKV + MLA_KV_RANK

MLA_Q_SCALE = (MLA_NOPE + MLA_ROPE) ** -0.5 * 1.4426950408889634

VMEM_LIMIT = 60 * 1024 * 1024
VMEM_LIMIT_WIDE = 63 * 1024 * 1024

f32 = jnp.float32
bf16 = jnp.bfloat16


def _rms(x, w):
    return x * lax.rsqrt(jnp.mean(x * x, axis=-1, keepdims=True) + EPS) * w


def _dot(a, b):
    return jnp.dot(a, b, preferred_element_type=f32)


def _dot_nt(a, b):
    return lax.dot_general(a, b, (((1,), (1,)), ((), ())), preferred_element_type=f32)


def _dot_tn(a, b):
    return lax.dot_general(a, b, (((0,), (0,)), ((), ())), preferred_element_type=f32)


def _resident(shape):
    nd = len(shape)
    return pl.BlockSpec(shape, lambda *_: (0,) * nd, pipeline_mode=pl.Buffered(1))


def _ffn_body(*refs, ts, norm_input, emit_next):
    refs = list(refs)
    h_ref = refs.pop(0)
    pre_ref = refs.pop(0)
    wgu_ref, wd_ref, post_ref = (refs.pop(0) for _ in range(3))
    tf = wd_ref.shape[0]
    npre_ref = refs.pop(0) if emit_next else None
    o_ref = refs.pop(0)
    xo_ref = refs.pop(0) if emit_next else None
    xn_ref = refs.pop(0) if norm_input else pre_ref
    j = pl.program_id(1)
    last = pl.num_programs(1) - 1

    def run(first, final):
        step_rows = ts if (first or final) else min(2 * ts, h_ref.shape[0])
        for r in range(h_ref.shape[0] // step_rows):
            rows = slice(r * step_rows, (r + 1) * step_rows)
            if norm_input and first:
                xn = _rms(h_ref[rows, :], pre_ref[...]).astype(bf16)
                xn_ref[rows, :] = xn
            else:
                xn = xn_ref[rows, :]
            gu = _dot(xn, wgu_ref[...])
            g, u = gu[:, :tf], gu[:, tf:]
            a = (g * jax.nn.sigmoid(g) * u).astype(bf16)
            acc = _dot(a, wd_ref[...])
            if not first:
                acc += o_ref[rows, :]
            if final:
                acc = h_ref[rows, :] + 0.5 * _rms(acc, post_ref[...])
                if emit_next:
                    xo_ref[rows, :] = _rms(acc, npre_ref[...]).astype(bf16)
            o_ref[rows, :] = acc

    pl.when(j == 0)(lambda: run(True, False))
    pl.when(jnp.logical_and(j > 0, j < last))(lambda: run(False, False))
    pl.when(j == last)(lambda: run(False, True))


def _gate_up_chunks(wg, wu, tf):
    n = D_FF // tf
    w = jnp.concatenate([wg.reshape(D_MODEL, n, tf), wu.reshape(D_MODEL, n, tf)], axis=-1)
    return w.reshape(D_MODEL, 2 * D_FF).astype(bf16)


def _ffn(h, pre, wgu, wd, post_w, next_pre_w=None, *, tm, ts, tf, vmem_limit=None):
    vmem_limit = VMEM_LIMIT if vmem_limit is None else vmem_limit
    n = h.shape[0]
    norm_input = pre.shape[0] == 1
    emit_next = next_pre_w is not None
    assert D_FF // tf >= 2 and tm % ts == 0
    row = pl.BlockSpec((tm, D_MODEL), lambda i, j: (i, 0))
    vec = pl.BlockSpec((1, D_MODEL), lambda i, j: (0, 0))
    in_specs = [
        row,
        vec if norm_input else row,
        pl.BlockSpec((D_MODEL, 2 * tf), lambda i, j: (0, j)),
        pl.BlockSpec((tf, D_MODEL), lambda i, j: (j, 0)),
        vec,
    ]
    args = [h, pre, wgu, wd, post_w]
    out_specs, out_shape = [row], [jax.ShapeDtypeStruct((n, D_MODEL), f32)]
    if emit_next:
        in_specs.append(vec)
        args.append(next_pre_w)
        out_specs.append(row)
        out_shape.append(jax.ShapeDtypeStruct((n, D_MODEL), bf16))
    out = pl.pallas_call(
        functools.partial(_ffn_body, ts=ts, norm_input=norm_input, emit_next=emit_next),
        grid=(n // tm, D_FF // tf),
        in_specs=in_specs,
        out_specs=out_specs,
        out_shape=out_shape,
        scratch_shapes=[pltpu.VMEM((tm, D_MODEL), bf16)] if norm_input else [],
        compiler_params=pltpu.CompilerParams(
            dimension_semantics=("parallel", "arbitrary"), vmem_limit_bytes=vmem_limit),
        name="swiglu",
    )(*args)
    return out if emit_next else out[0]


def _rot(x, c, s):
    return x * c + pltpu.roll(x, 64, 1) * s


def _mix_in_body(x_ref, win_ref, wkr_ref, qn_ref, wuqt_ref, kvn_ref, wuk_ref, wuvt_ref,
                 cr_ref, sr_ref, cm_ref, sm_ref, cmt_ref, smt_ref,
                 rq_ref, rk_ref, rv_ref, rg_ref, qt_ref, kc_ref, vt_ref, *, ts):
    k_scale = RET_DIM ** -0.5
    pair = 2 * RET_DIM
    half = LANES // 2
    for r in range(x_ref.shape[0] // ts):
        rows = slice(r * ts, (r + 1) * ts)
        xn = x_ref[rows, :]
        cr, sr = cr_ref[rows, :], sr_ref[rows, :]
        cm, sm = cm_ref[rows, :], sm_ref[rows, :]

        def proj(lo, width):
            return _dot(xn, win_ref[:, lo:lo + width])

        for lo in range(0, D_RET, pair):
            pq, pk = proj(_OFF_RQ + lo, pair), proj(_OFF_RK + lo, pair)
            for off in (0, RET_DIM):
                sl = slice(lo + off, lo + off + RET_DIM)
                rq_ref[rows, sl] = _rot(pq[:, off:off + RET_DIM], cr, sr).astype(bf16)
                rk_ref[rows, sl] = (
                    _rot(pk[:, off:off + RET_DIM], cr, sr) * k_scale).astype(bf16)
        for lo in range(0, D_RET, pair):
            rv_ref[rows, lo:lo + pair] = proj(_OFF_RV + lo, pair).astype(bf16)
            rg_ref[rows, lo:lo + pair] = proj(_OFF_RG + lo, pair)

        cq = _rms(proj(_OFF_CQ, MLA_Q_RANK), qn_ref[...]).astype(bf16)
        ckv = _rms(proj(_OFF_CKV, MLA_KV_RANK), kvn_ref[...]).astype(bf16)
        kr = _rot(_dot(xn, wkr_ref[...]), cm, sm).astype(bf16)
        cmt, smt = cmt_ref[:, rows], smt_ref[:, rows]
        for hd in range(MLA_HEADS):
            lo = hd * MLA_QK_PAD
            qt = _dot_nt(wuqt_ref[lo:lo + MLA_QK_PAD, :], cq)
            qr = qt[MLA_NOPE:]
            qr = qr * cmt + jnp.concatenate([qr[half:], qr[:half]], axis=0) * smt
            qt_ref[lo:lo + MLA_NOPE, rows] = (qt[:MLA_NOPE] * MLA_Q_SCALE).astype(bf16)
            qt_ref[lo + MLA_NOPE:lo + MLA_QK_PAD, rows] = (qr * MLA_Q_SCALE).astype(bf16)
            kc_ref[rows, lo + MLA_NOPE:lo + MLA_QK_PAD] = kr
        for lo in range(0, D_MLA, 2 * MLA_V):
            kn = _dot(ckv, wuk_ref[:, lo:lo + 2 * MLA_V])
            for off in (0, MLA_NOPE):
                hd = (lo + off) // MLA_NOPE
                kc_ref[rows, hd * MLA_QK_PAD:hd * MLA_QK_PAD + MLA_NOPE] = (
                    kn[:, off:off + MLA_NOPE].astype(bf16))
            vt_ref[0, lo:lo + 2 * MLA_V, rows] = _dot_nt(wuvt_ref[lo:lo + 2 * MLA_V, :],
                                                         ckv).astype(bf16)


def _mix_in(xn, win, wkr, qn_w, wuqt, kvn_w, wuk, wuvt, cr, sr, cm, sm, cmt, smt, *, tm, ts):
    n = xn.shape[0]
    n_pos_blocks = cr.shape[0] // tm

    def row(width):
        return pl.BlockSpec((tm, width), lambda i: (i, 0))

    tab = pl.BlockSpec((tm, LANES), lambda i: (i % n_pos_blocks, 0))
    tab_t = pl.BlockSpec((LANES, tm), lambda i: (0, i % n_pos_blocks))
    d_qk = MLA_HEADS * MLA_QK_PAD
    out_shape = [
        jax.ShapeDtypeStruct((n, D_RET), bf16),
        jax.ShapeDtypeStruct((n, D_RET), bf16),
        jax.ShapeDtypeStruct((n, D_RET), bf16),
        jax.ShapeDtypeStruct((n, D_RET), f32),
        jax.ShapeDtypeStruct((d_qk, n), bf16),
        jax.ShapeDtypeStruct((n, d_qk), bf16),
        jax.ShapeDtypeStruct((n // tm, D_MLA, tm), bf16),
    ]
    return pl.pallas_call(
        functools.partial(_mix_in_body, ts=ts),
        grid=(n // tm,),
        in_specs=[
            row(D_MODEL),
            _resident(win.shape),
            _resident(wkr.shape),
            _resident((1, MLA_Q_RANK)),
            _resident(wuqt.shape),
            _resident((1, MLA_KV_RANK)),
            _resident(wuk.shape),
            _resident(wuvt.shape),
            tab, tab, tab, tab, tab_t, tab_t,
        ],
        out_specs=[row(D_RET), row(D_RET), row(D_RET), row(D_RET),
                   pl.BlockSpec((d_qk, tm), lambda i: (0, i)),
                   row(d_qk),
                   pl.BlockSpec((1, D_MLA, tm), lambda i: (i, 0, 0))],
        out_shape=out_shape,
        compiler_params=pltpu.CompilerParams(
            dimension_semantics=("parallel",), vmem_limit_bytes=VMEM_LIMIT),
        name="mix_in",
    )(xn, win, wkr, qn_w, wuqt, kvn_w, wuk, wuvt, cr, sr, cm, sm, cmt, smt)


def _retention_body(lg_ref, rq_ref, rk_ref, rv_ref, rg_ref, gn_ref, km_ref, vm_ref, o_ref,
                    dmask_ref, xi_ref, zeta_ref, state_ref, *, rows):
    t = pl.program_id(1)

    @pl.when(t == 0)
    def _():
        row = lax.broadcasted_iota(jnp.int32, (CHUNK, CHUNK), 0).astype(f32)
        col = lax.broadcasted_iota(jnp.int32, (CHUNK, CHUNK), 1).astype(f32)
        diff = row - col
        mrow = lax.broadcasted_iota(jnp.int32, (N_META, RET_DIM), 0).astype(f32)
        for hd in range(RET_HEADS):
            lg = lg_ref[hd]
            sl = slice(hd * RET_DIM, (hd + 1) * RET_DIM)
            dmask_ref[hd] = jnp.where(diff >= 0, jnp.exp(jnp.maximum(diff, 0.0) * lg), 0.0)
            xi_ref[hd] = jnp.exp((row + 1.0) * lg)
            zeta_ref[hd] = jnp.exp((CHUNK - 1.0 - row) * lg)
            zeta_m = jnp.exp((N_META - 1.0 - mrow) * lg)
            state_ref[hd] = _dot_tn((km_ref[:, sl].astype(f32) * zeta_m).astype(bf16),
                                    vm_ref[:, sl])

    chunks = [slice(c * CHUNK, (c + 1) * CHUNK) for c in range(rows // CHUNK)]

    def head_cols(hd):
        return slice(hd * RET_DIM, (hd + 1) * RET_DIM)

    def state_free(hd):
        sl = head_cols(hd)
        dmask, zeta = dmask_ref[hd], zeta_ref[hd]
        out = []
        for r in chunks:
            q, k, v = rq_ref[r, sl], rk_ref[r, sl], rv_ref[r, sl]
            s = (_dot_nt(q, k) * dmask).astype(bf16)
            inc = _dot_tn((k.astype(f32) * zeta).astype(bf16), v)
            out.append((s, inc))
        return out

    def state_bound(hd, staged):
        sl = head_cols(hd)
        g_chunk = jnp.exp(CHUNK * lg_ref[hd])
        xi, gn = xi_ref[hd], gn_ref[:, sl]
        state = state_ref[hd]
        for r, (s, inc) in zip(chunks, staged):
            o = _dot(s, rv_ref[r, sl]) + _dot(rq_ref[r, sl], state.astype(bf16)) * xi
            state = state * g_chunk + inc
            g = rg_ref[r, sl]
            o_ref[r, sl] = (_rms(o, gn) * (g * jax.nn.sigmoid(g))).astype(bf16)
        state_ref[hd] = state

    staged = state_free(0)
    for hd in range(RET_HEADS):
        ahead = state_free(hd + 1) if hd + 1 < RET_HEADS else None
        state_bound(hd, staged)
        staged = ahead


def _retention(log_g, rq, rk, rv, rg, gn_w, rk_meta, rv_meta, *, batch, seq, rows):
    steps = seq // rows
    blk = pl.BlockSpec((rows, D_RET), lambda b, t: (b * steps + t, 0))
    head_tab = pltpu.VMEM((RET_HEADS, CHUNK, CHUNK), f32)
    return pl.pallas_call(
        functools.partial(_retention_body, rows=rows),
        grid=(batch, steps),
        in_specs=[
            _resident((RET_HEADS, 1, LANES)),
            blk, blk, blk, blk,
            _resident((1, D_RET)),
            _resident((N_META, D_RET)),
            _resident((N_META, D_RET)),
        ],
        out_specs=blk,
        out_shape=jax.ShapeDtypeStruct((batch * seq, D_RET), bf16),
        scratch_shapes=[head_tab, head_tab, head_tab,
                        pltpu.VMEM((RET_HEADS, RET_DIM, RET_DIM), f32)],
        compiler_params=pltpu.CompilerParams(
            dimension_semantics=("parallel", "arbitrary"), vmem_limit_bytes=VMEM_LIMIT),
        name="retention",
    )(log_g, rq, rk, rv, rg, gn_w, rk_meta, rv_meta)


def _mla_body(qt_ref, k_ref, vt_ref, km_ref, vmt_ref, o_ref,
              s_ref, m_ref, l_ref, acc_ref, *, tq, hg):
    qi = pl.program_id(2)
    heads = range(hg)

    def qk_cols(g):
        return slice(g * MLA_QK_PAD, (g + 1) * MLA_QK_PAD)

    def v_cols(g):
        return slice(g * MLA_V, (g + 1) * MLA_V)

    def meta_scores():
        return [_dot(km_ref[:, qk_cols(g)], qt_ref[qk_cols(g), :]) for g in heads]

    def meta_keys(s_all):
        for g in heads:
            s = s_all[g]
            m0 = jnp.max(s, axis=0, keepdims=True)
            p = jnp.exp2(s - m0)
            m_ref[g] = m0
            l_ref[g] = jnp.sum(p, axis=0, keepdims=True)
            acc_ref[g] = _dot(vmt_ref[v_cols(g), :], p.astype(bf16))

    def scores(j, slot):
        r = pl.ds(pl.multiple_of(j * tq, tq), tq)
        for g in heads:
            s_ref[slot, g] = _dot(k_ref[r, qk_cols(g)], qt_ref[qk_cols(g), :])

    def block(j, slot, masked):
        for g in heads:
            s = s_ref[slot, g]
            if masked:
                kpos = lax.broadcasted_iota(jnp.int32, (tq, tq), 0)
                qpos = lax.broadcasted_iota(jnp.int32, (tq, tq), 1)
                s = jnp.where(kpos <= qpos, s, -1e30)
            m_old = m_ref[g]
            m_new = jnp.maximum(m_old, jnp.max(s, axis=0, keepdims=True))
            alpha = jnp.exp2(m_old - m_new)
            p = jnp.exp2(s - m_new)
            l_ref[g] = alpha * l_ref[g] + jnp.sum(p, axis=0, keepdims=True)
            acc_ref[g] = alpha * acc_ref[g] + _dot(vt_ref[j, v_cols(g), :], p.astype(bf16))
            m_ref[g] = m_new

    def step(j, slot):
        scores(j + 1, 1 - slot)
        block(j, slot, masked=False)

    def pair(i, carry):
        step(2 * i, 0)
        step(2 * i + 1, 1)
        return carry

    s_meta = meta_scores()
    scores(0, 0)
    meta_keys(s_meta)
    lax.fori_loop(0, lax.shift_right_logical(qi, 1), pair, 0)
    odd = (qi & 1) == 1

    @pl.when(odd)
    def _():
        step(qi - 1, 0)
        block(qi, 1, masked=True)

    @pl.when(jnp.logical_not(odd))
    def _():
        block(qi, 0, masked=True)

    for g in heads:
        o_ref[:, v_cols(g)] = (acc_ref[g] / l_ref[g]).T.astype(bf16)


def _mla(qt, kc, vt, kc_meta, vt_meta, *, batch, seq, hg):
    tq = vt.shape[-1]
    nq = seq // tq
    return pl.pallas_call(
        functools.partial(_mla_body, tq=tq, hg=hg),
        grid=(batch, MLA_HEADS // hg, nq),
        in_specs=[
            pl.BlockSpec((hg * MLA_QK_PAD, tq), lambda b, h, i: (h, b * nq + i)),
            pl.BlockSpec((seq, hg * MLA_QK_PAD), lambda b, h, i: (b, h)),
            pl.BlockSpec((nq, hg * MLA_V, tq), lambda b, h, i: (b, h, 0)),
            pl.BlockSpec((N_META, hg * MLA_QK_PAD), lambda b, h, i: (0, h)),
            pl.BlockSpec((hg * MLA_V, N_META), lambda b, h, i: (h, 0)),
        ],
        out_specs=pl.BlockSpec((tq, hg * MLA_V), lambda b, h, i: (b * nq + i, h)),
        out_shape=jax.ShapeDtypeStruct((batch * seq, D_MLA), bf16),
        scratch_shapes=[
            pltpu.VMEM((2, hg, tq, tq), f32),
            pltpu.VMEM((hg, 1, tq), f32),
            pltpu.VMEM((hg, 1, tq), f32),
            pltpu.VMEM((hg, MLA_V, tq), f32),
        ],
        compiler_params=pltpu.CompilerParams(
            dimension_semantics=("parallel", "parallel", "arbitrary"),
            vmem_limit_bytes=VMEM_LIMIT),
        name="mla_attention",
    )(qt, kc, vt, kc_meta, vt_meta)


def _mix_out_body(ret_ref, mla_ref, wo_ref, h_ref, post_ref, npre_ref, o_ref, xo_ref, *, ts):
    for r in range(h_ref.shape[0] // ts):
        rows = slice(r * ts, (r + 1) * ts)
        m = (_dot(ret_ref[rows, :], wo_ref[:D_RET, :]) +
             _dot(mla_ref[rows, :], wo_ref[D_RET:, :]))
        hn = h_ref[rows, :] + _rms(m, post_ref[...])
        o_ref[rows, :] = hn
        xo_ref[rows, :] = _rms(hn, npre_ref[...]).astype(bf16)


def _mix_out(ret, mla, wo, h, post_w, next_pre_w, *, tm, ts):
    n = h.shape[0]
    row = pl.BlockSpec((tm, D_MODEL), lambda i: (i, 0))
    return pl.pallas_call(
        functools.partial(_mix_out_body, ts=ts),
        grid=(n // tm,),
        in_specs=[
            pl.BlockSpec((tm, D_RET), lambda i: (i, 0)),
            pl.BlockSpec((tm, D_MLA), lambda i: (i, 0)),
            _resident(wo.shape),
            row,
            _resident((1, D_MODEL)),
            _resident((1, D_MODEL)),
        ],
        out_specs=[row, row],
        out_shape=[jax.ShapeDtypeStruct((n, D_MODEL), f32),
                   jax.ShapeDtypeStruct((n, D_MODEL), bf16)],
        compiler_params=pltpu.CompilerParams(
            dimension_semantics=("parallel",), vmem_limit_bytes=VMEM_LIMIT),
        name="mix_out",
    )(ret, mla, wo, h, post_w, next_pre_w)


def _rope_tables(seq_len):
    pos = np.arange(seq_len, dtype=np.float64)

    def cos_sin(dim):
        inv = ROPE_THETA ** (-np.arange(0, dim, 2, dtype=np.float64) / dim)
        ang = pos[:, None] * inv[None, :]
        return np.cos(ang), np.sin(ang)

    c, s = cos_sin(RET_DIM)
    cr = np.concatenate([c, c], axis=-1)
    sr = np.concatenate([-s, s], axis=-1)
    c, s = cos_sin(MLA_ROPE)
    z = np.zeros_like(c)
    cm = np.concatenate([c, z, c, z], axis=-1)
    sm = np.concatenate([-s, z, s, z], axis=-1)
    return tuple(t.astype(np.float32) for t in (cr, sr, cm, sm))


def _pad_rope_cols(w):
    half = MLA_ROPE // 2
    z = jnp.zeros(w.shape[:-1] + (half,), w.dtype)
    return jnp.concatenate([w[..., :half], z, w[..., half:], z], axis=-1)


def kernel(x, meta_tokens, ffn1_pre_norm, ffn1_w_gate, ffn1_w_up, ffn1_w_down, ffn1_post_norm, mix_pre_norm, w_in, ret_group_norm, mla_q_norm, mla_w_uq, mla_kv_norm, mla_w_uk, mla_w_uv, w_out, mix_post_norm, ffn2_pre_norm, ffn2_w_gate, ffn2_w_up, ffn2_w_down, ffn2_post_norm):
    batch, seq, _ = x.shape
    assert ffn1_pre_norm.shape[0] == 1, "single-layer trunk only"
    l = 0
    cr, sr, cm, sm = _rope_tables(N_META + seq)
    tabs_meta = tuple(t[:N_META] for t in (cr, sr, cm, sm)) + (cm[:N_META].T, sm[:N_META].T)
    tabs_real = tuple(t[N_META:] for t in (cr, sr, cm, sm)) + (cm[N_META:].T, sm[N_META:].T)
    log_g = jnp.log(1.0 - 2.0 ** (-5.0 - jnp.arange(RET_HEADS, dtype=f32)))
    log_g = jnp.broadcast_to(log_g[:, None, None], (RET_HEADS, 1, LANES))

    h = x.reshape(batch * seq, D_MODEL)
    hm = meta_tokens.astype(x.dtype)
    vec = lambda w: w[l].reshape(1, -1)
    tf = 512
    ffn1 = (vec(ffn1_pre_norm), _gate_up_chunks(ffn1_w_gate[l], ffn1_w_up[l], tf),
            ffn1_w_down[l].astype(bf16), vec(ffn1_post_norm))
    ffn2 = (vec(ffn2_pre_norm), _gate_up_chunks(ffn2_w_gate[l], ffn2_w_up[l], tf),
            ffn2_w_down[l].astype(bf16), vec(ffn2_post_norm))
    win = w_in[l][:, :_OFF_KR].astype(bf16)
    wkr = _pad_rope_cols(w_in[l][:, _OFF_KR:]).astype(bf16)
    wuq = mla_w_uq[l].reshape(MLA_Q_RANK, MLA_HEADS, MLA_NOPE + MLA_ROPE)
    wuq = jnp.concatenate([wuq[..., :MLA_NOPE], _pad_rope_cols(wuq[..., MLA_NOPE:])], axis=-1)
    wuqt = wuq.reshape(MLA_Q_RANK, MLA_HEADS * MLA_QK_PAD).T.astype(bf16)
    mix_w = (win, wkr, vec(mla_q_norm), wuqt, vec(mla_kv_norm),
             mla_w_uk[l].astype(bf16), mla_w_uv[l].T.astype(bf16))
    wo = w_out[l].astype(bf16)

    h, xn = _ffn(h, *ffn1, vec(mix_pre_norm), tm=1024, ts=256, tf=tf,
                 vmem_limit=VMEM_LIMIT_WIDE)
    _, xn_m = _ffn(hm, *ffn1, vec(mix_pre_norm), tm=N_META, ts=N_META, tf=tf)
    _, rk_m, rv_m, _, _, kc_m, vt_m = _mix_in(xn_m, *mix_w, *tabs_meta, tm=N_META, ts=N_META)
    rq, rk, rv, rg, qt, kc, vt = _mix_in(xn, *mix_w, *tabs_real, tm=512, ts=256)
    ret = _retention(log_g, rq, rk, rv, rg, vec(ret_group_norm), rk_m, rv_m,
                     batch=batch, seq=seq, rows=512)
    mla = _mla(qt, kc, vt, kc_m, vt_m[0], batch=batch, seq=seq, hg=4)
    h, xn = _mix_out(ret, mla, wo, h, vec(mix_post_norm), ffn2[0], tm=512, ts=128)
    h = _ffn(h, xn, *ffn2[1:], tm=1024, ts=256, tf=tf)
    return h.reshape(batch, seq, D_MODEL)
```

```python
import functools

import jax
import jax.numpy as jnp
import numpy as np
from jax import lax
from jax.experimental import pallas as pl
from jax.experimental.pallas import tpu as pltpu

D_MODEL = 2048
N_META = 16
CHUNK = 128
RET_HEADS = 8
RET_DIM = 128
MLA_HEADS = 8
MLA_NOPE = 128
MLA_ROPE = 64
MLA_V = 128
MLA_Q_RANK = 512
MLA_KV_RANK = 256
MLA_QK_PAD = 256
D_RET = RET_HEADS * RET_DIM
D_MLA = MLA_HEADS * MLA_V
D_FF = 5632
ROPE_THETA = 10000.0
EPS = 1e-6
LANES = 128

_OFF_RQ = 0
_OFF_RK = D_RET
_OFF_RV = 2 * D_RET
_OFF_RG = 3 * D_RET
_OFF_CQ = 4 * D_RET
_OFF_CKV = _OFF_CQ + MLA_Q_RANK
_OFF_KR = _OFF_CKV + MLA_KV_RANK

MLA_Q_SCALE = (MLA_NOPE + MLA_ROPE) ** -0.5 * 1.4426950408889634

VMEM_LIMIT = 60 * 1024 * 1024
VMEM_LIMIT_WIDE = 63 * 1024 * 1024

f32 = jnp.float32
bf16 = jnp.bfloat16


def _rms(x, w):
    return x * lax.rsqrt(jnp.mean(x * x, axis=-1, keepdims=True) + EPS) * w


def _dot(a, b):
    return jnp.dot(a, b, preferred_element_type=f32)


def _dot_nt(a, b):
    return lax.dot_general(a, b, (((1,), (1,)), ((), ())), preferred_element_type=f32)


def _dot_tn(a, b):
    return lax.dot_general(a, b, (((0,), (0,)), ((), ())), preferred_element_type=f32)


def _resident(shape):
    nd = len(shape)
    return pl.BlockSpec(shape, lambda *_: (0,) * nd, pipeline_mode=pl.Buffered(1))


def _ffn_body(*refs, ts, norm_input, emit_next):
    refs = list(refs)
    h_ref = refs.pop(0)
    pre_ref = refs.pop(0)
    wg_ref, wu_ref, wd_ref, post_ref = (refs.pop(0) for _ in range(4))
    npre_ref = refs.pop(0) if emit_next else None
    o_ref = refs.pop(0)
    xo_ref = refs.pop(0) if emit_next else None
    xn_ref = refs.pop(0) if norm_input else pre_ref
    j = pl.program_id(1)
    last = pl.num_programs(1) - 1

    def run(first, final):
        step_rows = ts if (first or final) else min(2 * ts, h_ref.shape[0])
        for r in range(h_ref.shape[0] // step_rows):
            rows = slice(r * step_rows, (r + 1) * step_rows)
            if norm_input and first:
                xn = _rms(h_ref[rows, :], pre_ref[...]).astype(bf16)
                xn_ref[rows, :] = xn
            else:
                xn = xn_ref[rows, :]
            g = _dot(xn, wg_ref[...])
            u = _dot(xn, wu_ref[...])
            a = (g * jax.nn.sigmoid(g) * u).astype(bf16)
            acc = _dot(a, wd_ref[...])
            if not first:
                acc += o_ref[rows, :]
            if final:
                acc = h_ref[rows, :] + 0.5 * _rms(acc, post_ref[...])
                if emit_next:
                    xo_ref[rows, :] = _rms(acc, npre_ref[...]).astype(bf16)
            o_ref[rows, :] = acc

    pl.when(j == 0)(lambda: run(True, False))
    pl.when(jnp.logical_and(j > 0, j < last))(lambda: run(False, False))
    pl.when(j == last)(lambda: run(False, True))


def _ffn(h, pre, wg, wu, wd, post_w, next_pre_w=None, *, tm, ts, tf, vmem_limit=None):
    vmem_limit = VMEM_LIMIT if vmem_limit is None else vmem_limit
    n = h.shape[0]
    norm_input = pre.shape[0] == 1
    emit_next = next_pre_w is not None
    assert D_FF // tf >= 2 and tm % ts == 0
    row = pl.BlockSpec((tm, D_MODEL), lambda i, j: (i, 0))
    vec = pl.BlockSpec((1, D_MODEL), lambda i, j: (0, 0))
    in_specs = [
        row,
        vec if norm_input else row,
        pl.BlockSpec((D_MODEL, tf), lambda i, j: (0, j)),
        pl.BlockSpec((D_MODEL, tf), lambda i, j: (0, j)),
        pl.BlockSpec((tf, D_MODEL), lambda i, j: (j, 0)),
        vec,
    ]
    args = [h, pre, wg, wu, wd, post_w]
    out_specs, out_shape = [row], [jax.ShapeDtypeStruct((n, D_MODEL), f32)]
    if emit_next:
        in_specs.append(vec)
        args.append(next_pre_w)
        out_specs.append(row)
        out_shape.append(jax.ShapeDtypeStruct((n, D_MODEL), bf16))
    out = pl.pallas_call(
        functools.partial(_ffn_body, ts=ts, norm_input=norm_input, emit_next=emit_next),
        grid=(n // tm, D_FF // tf),
        in_specs=in_specs,
        out_specs=out_specs,
        out_shape=out_shape,
        scratch_shapes=[pltpu.VMEM((tm, D_MODEL), bf16)] if norm_input else [],
        compiler_params=pltpu.CompilerParams(
            dimension_semantics=("parallel", "arbitrary"), vmem_limit_bytes=vmem_limit),
        name="swiglu",
    )(*args)
    return out if emit_next else out[0]


def _rot(x, c, s):
    return x * c + pltpu.roll(x, 64, 1) * s


def _mix_in_body(x_ref, win_ref, wkr_ref, qn_ref, wuqt_ref, kvn_ref, wuk_ref, wuvt_ref,
                 cr_ref, sr_ref, cm_ref, sm_ref, cmt_ref, smt_ref,
                 rq_ref, rk_ref, rv_ref, rg_ref, qt_ref, kc_ref, vt_ref, *, ts):
    k_scale = RET_DIM ** -0.5
    pair = 2 * RET_DIM
    half = LANES // 2
    for r in range(x_ref.shape[0] // ts):
        rows = slice(r * ts, (r + 1) * ts)
        xn = x_ref[rows, :]
        cr, sr = cr_ref[rows, :], sr_ref[rows, :]
        cm, sm = cm_ref[rows, :], sm_ref[rows, :]

        def proj(lo, width):
            return _dot(xn, win_ref[:, lo:lo + width])

        for lo in range(0, D_RET, pair):
            pq, pk = proj(_OFF_RQ + lo, pair), proj(_OFF_RK + lo, pair)
            for off in (0, RET_DIM):
                sl = slice(lo + off, lo + off + RET_DIM)
                rq_ref[rows, sl] = _rot(pq[:, off:off + RET_DIM], cr, sr).astype(bf16)
                rk_ref[rows, sl] = (
                    _rot(pk[:, off:off + RET_DIM], cr, sr) * k_scale).astype(bf16)
        for lo in range(0, D_RET, pair):
            rv_ref[rows, lo:lo + pair] = proj(_OFF_RV + lo, pair).astype(bf16)
            rg_ref[rows, lo:lo + pair] = proj(_OFF_RG + lo, pair)

        cq = _rms(proj(_OFF_CQ, MLA_Q_RANK), qn_ref[...]).astype(bf16)
        ckv = _rms(proj(_OFF_CKV, MLA_KV_RANK), kvn_ref[...]).astype(bf16)
        kr = _rot(_dot(xn, wkr_ref[...]), cm, sm).astype(bf16)
        cmt, smt = cmt_ref[:, rows], smt_ref[:, rows]
        for hd in range(MLA_HEADS):
            lo = hd * MLA_QK_PAD
            qt = _dot_nt(wuqt_ref[lo:lo + MLA_QK_PAD, :], cq)
            qr = qt[MLA_NOPE:]
            qr = qr * cmt + jnp.concatenate([qr[half:], qr[:half]], axis=0) * smt
            qt_ref[lo:lo + MLA_NOPE, rows] = (qt[:MLA_NOPE] * MLA_Q_SCALE).astype(bf16)
            qt_ref[lo + MLA_NOPE:lo + MLA_QK_PAD, rows] = (qr * MLA_Q_SCALE).astype(bf16)
            kc_ref[rows, lo + MLA_NOPE:lo + MLA_QK_PAD] = kr
        for lo in range(0, D_MLA, 2 * MLA_V):
            kn = _dot(ckv, wuk_ref[:, lo:lo + 2 * MLA_V])
            for off in (0, MLA_NOPE):
                hd = (lo + off) // MLA_NOPE
                kc_ref[rows, hd * MLA_QK_PAD:hd * MLA_QK_PAD + MLA_NOPE] = (
                    kn[:, off:off + MLA_NOPE].astype(bf16))
            vt_ref[0, lo:lo + 2 * MLA_V, rows] = _dot_nt(wuvt_ref[lo:lo + 2 * MLA_V, :],
                                                         ckv).astype(bf16)


def _mix_in(xn, win, wkr, qn_w, wuqt, kvn_w, wuk, wuvt, cr, sr, cm, sm, cmt, smt, *, tm, ts):
    n = xn.shape[0]
    n_pos_blocks = cr.shape[0] // tm

    def row(width):
        return pl.BlockSpec((tm, width), lambda i: (i, 0))

    tab = pl.BlockSpec((tm, LANES), lambda i: (i % n_pos_blocks, 0))
    tab_t = pl.BlockSpec((LANES, tm), lambda i: (0, i % n_pos_blocks))
    d_qk = MLA_HEADS * MLA_QK_PAD
    out_shape = [
        jax.ShapeDtypeStruct((n, D_RET), bf16),
        jax.ShapeDtypeStruct((n, D_RET), bf16),
        jax.ShapeDtypeStruct((n, D_RET), bf16),
        jax.ShapeDtypeStruct((n, D_RET), f32),
        jax.ShapeDtypeStruct((d_qk, n), bf16),
        jax.ShapeDtypeStruct((n, d_qk), bf16),
        jax.ShapeDtypeStruct((n // tm, D_MLA, tm), bf16),
    ]
    return pl.pallas_call(
        functools.partial(_mix_in_body, ts=ts),
        grid=(n // tm,),
        in_specs=[
            row(D_MODEL),
            _resident(win.shape),
            _resident(wkr.shape),
            _resident((1, MLA_Q_RANK)),
            _resident(wuqt.shape),
            _resident((1, MLA_KV_RANK)),
            _resident(wuk.shape),
            _resident(wuvt.shape),
            tab, tab, tab, tab, tab_t, tab_t,
        ],
        out_specs=[row(D_RET), row(D_RET), row(D_RET), row(D_RET),
                   pl.BlockSpec((d_qk, tm), lambda i: (0, i)),
                   row(d_qk),
                   pl.BlockSpec((1, D_MLA, tm), lambda i: (i, 0, 0))],
        out_shape=out_shape,
        compiler_params=pltpu.CompilerParams(
            dimension_semantics=("parallel",), vmem_limit_bytes=VMEM_LIMIT),
        name="mix_in",
    )(xn, win, wkr, qn_w, wuqt, kvn_w, wuk, wuvt, cr, sr, cm, sm, cmt, smt)


def _retention_body(lg_ref, rq_ref, rk_ref, rv_ref, rg_ref, gn_ref, km_ref, vm_ref, o_ref,
                    dmask_ref, xi_ref, zeta_ref, state_ref, *, rows):
    t = pl.program_id(1)

    @pl.when(t == 0)
    def _():
        row = lax.broadcasted_iota(jnp.int32, (CHUNK, CHUNK), 0).astype(f32)
        col = lax.broadcasted_iota(jnp.int32, (CHUNK, CHUNK), 1).astype(f32)
        diff = row - col
        mrow = lax.broadcasted_iota(jnp.int32, (N_META, RET_DIM), 0).astype(f32)
        for hd in range(RET_HEADS):
            lg = lg_ref[hd]
            sl = slice(hd * RET_DIM, (hd + 1) * RET_DIM)
            dmask_ref[hd] = jnp.where(diff >= 0, jnp.exp(jnp.maximum(diff, 0.0) * lg), 0.0)
            xi_ref[hd] = jnp.exp((row + 1.0) * lg)
            zeta_ref[hd] = jnp.exp((CHUNK - 1.0 - row) * lg)
            zeta_m = jnp.exp((N_META - 1.0 - mrow) * lg)
            state_ref[hd] = _dot_tn((km_ref[:, sl].astype(f32) * zeta_m).astype(bf16),
                                    vm_ref[:, sl])

    chunks = [slice(c * CHUNK, (c + 1) * CHUNK) for c in range(rows // CHUNK)]

    def head_cols(hd):
        return slice(hd * RET_DIM, (hd + 1) * RET_DIM)

    def state_free(hd):
        sl = head_cols(hd)
        dmask, zeta = dmask_ref[hd], zeta_ref[hd]
        out = []
        for r in chunks:
            q, k, v = rq_ref[r, sl], rk_ref[r, sl], rv_ref[r, sl]
            s = (_dot_nt(q, k) * dmask).astype(bf16)
            inc = _dot_tn((k.astype(f32) * zeta).astype(bf16), v)
            out.append((s, inc))
        return out

    def state_bound(hd, staged):
        sl = head_cols(hd)
        g_chunk = jnp.exp(CHUNK * lg_ref[hd])
        xi, gn = xi_ref[hd], gn_ref[:, sl]
        state = state_ref[hd]
        for r, (s, inc) in zip(chunks, staged):
            o = _dot(s, rv_ref[r, sl]) + _dot(rq_ref[r, sl], state.astype(bf16)) * xi
            state = state * g_chunk + inc
            g = rg_ref[r, sl]
            o_ref[r, sl] = (_rms(o, gn) * (g * jax.nn.sigmoid(g))).astype(bf16)
        state_ref[hd] = state

    staged = state_free(0)
    for hd in range(RET_HEADS):
        ahead = state_free(hd + 1) if hd + 1 < RET_HEADS else None
        state_bound(hd, staged)
        staged = ahead


def _retention(log_g, rq, rk, rv, rg, gn_w, rk_meta, rv_meta, *, batch, seq, rows):
    steps = seq // rows
    blk = pl.BlockSpec((rows, D_RET), lambda b, t: (b * steps + t, 0))
    head_tab = pltpu.VMEM((RET_HEADS, CHUNK, CHUNK), f32)
    return pl.pallas_call(
        functools.partial(_retention_body, rows=rows),
        grid=(batch, steps),
        in_specs=[
            _resident((RET_HEADS, 1, LANES)),
            blk, blk, blk, blk,
            _resident((1, D_RET)),
            _resident((N_META, D_RET)),
            _resident((N_META, D_RET)),
        ],
        out_specs=blk,
        out_shape=jax.ShapeDtypeStruct((batch * seq, D_RET), bf16),
        scratch_shapes=[head_tab, head_tab, head_tab,
                        pltpu.VMEM((RET_HEADS, RET_DIM, RET_DIM), f32)],
        compiler_params=pltpu.CompilerParams(
            dimension_semantics=("parallel", "arbitrary"), vmem_limit_bytes=VMEM_LIMIT),
        name="retention",
    )(log_g, rq, rk, rv, rg, gn_w, rk_meta, rv_meta)


def _mla_body(qt_ref, k_ref, vt_ref, km_ref, vmt_ref, *rest, tq, hg, n_casts):
    cast_src, o_ref, cast_dst = rest[:n_casts], rest[n_casts], rest[n_casts + 1:2 * n_casts + 1]
    s_ref, m_ref, l_ref, acc_ref = rest[2 * n_casts + 1:]
    qi = pl.program_id(2)
    heads = range(hg)

    def qk_cols(g):
        return slice(g * MLA_QK_PAD, (g + 1) * MLA_QK_PAD)

    def v_cols(g):
        return slice(g * MLA_V, (g + 1) * MLA_V)

    def side_casts():
        for src, dst in zip(cast_src, cast_dst):
            dst[...] = src[...].astype(bf16)

    def meta_scores():
        return [_dot(km_ref[:, qk_cols(g)], qt_ref[qk_cols(g), :]) for g in heads]

    def meta_keys(s_all):
        for g in heads:
            s = s_all[g]
            m0 = jnp.max(s, axis=0, keepdims=True)
            p = jnp.exp2(s - m0)
            m_ref[g] = m0
            l_ref[g] = jnp.sum(p, axis=0, keepdims=True)
            acc_ref[g] = _dot(vmt_ref[v_cols(g), :], p.astype(bf16))

    def scores(j, slot):
        r = pl.ds(pl.multiple_of(j * tq, tq), tq)
        for g in heads:
            s_ref[slot, g] = _dot(k_ref[r, qk_cols(g)], qt_ref[qk_cols(g), :])

    def block(j, slot, masked):
        for g in heads:
            s = s_ref[slot, g]
            if masked:
                kpos = lax.broadcasted_iota(jnp.int32, (tq, tq), 0)
                qpos = lax.broadcasted_iota(jnp.int32, (tq, tq), 1)
                s = jnp.where(kpos <= qpos, s, -1e30)
            m_old = m_ref[g]
            m_new = jnp.maximum(m_old, jnp.max(s, axis=0, keepdims=True))
            alpha = jnp.exp2(m_old - m_new)
            p = jnp.exp2(s - m_new)
            l_ref[g] = alpha * l_ref[g] + jnp.sum(p, axis=0, keepdims=True)
            acc_ref[g] = alpha * acc_ref[g] + _dot(vt_ref[j, v_cols(g), :], p.astype(bf16))
            m_ref[g] = m_new

    def step(j, slot):
        scores(j + 1, 1 - slot)
        block(j, slot, masked=False)

    def pair(i, carry):
        step(2 * i, 0)
        step(2 * i + 1, 1)
        return carry

    s_meta = meta_scores()
    scores(0, 0)
    meta_keys(s_meta)
    lax.fori_loop(0, lax.shift_right_logical(qi, 1), pair, 0)
    odd = (qi & 1) == 1

    @pl.when(odd)
    def _():
        side_casts()
        step(qi - 1, 0)
        block(qi, 1, masked=True)

    @pl.when(jnp.logical_not(odd))
    def _():
        side_casts()
        block(qi, 0, masked=True)

    for g in heads:
        o_ref[:, v_cols(g)] = (acc_ref[g] / l_ref[g]).T.astype(bf16)


def _cast_block_spec(shape, grid):
    steps = int(np.prod(grid))
    rows, cols = shape
    for col_parts in (1, 2, 4, 8, 16):
        row_parts = steps // col_parts
        if (steps % col_parts == 0 and rows % (16 * row_parts) == 0
                and cols % (LANES * col_parts) == 0):
            break
    else:
        raise ValueError(f"no per-step blocking of {shape} over {steps} grid steps")

    def index_map(*idx):
        step = 0
        for i, n in zip(idx, grid):
            step = step * n + i
        return step // col_parts, step % col_parts

    return pl.BlockSpec((rows // row_parts, cols // col_parts), index_map)


def _mla(qt, kc, vt, kc_meta, vt_meta, weights_to_cast=(), *, batch, seq, hg):
    tq = vt.shape[-1]
    nq = seq // tq
    grid = (batch, MLA_HEADS // hg, nq)
    cast_specs = [_cast_block_spec(w.shape, grid) for w in weights_to_cast]
    return pl.pallas_call(
        functools.partial(_mla_body, tq=tq, hg=hg, n_casts=len(cast_specs)),
        grid=grid,
        in_specs=[
            pl.BlockSpec((hg * MLA_QK_PAD, tq), lambda b, h, i: (h, b * nq + i)),
            pl.BlockSpec((seq, hg * MLA_QK_PAD), lambda b, h, i: (b, h)),
            pl.BlockSpec((nq, hg * MLA_V, tq), lambda b, h, i: (b, h, 0)),
            pl.BlockSpec((N_META, hg * MLA_QK_PAD), lambda b, h, i: (0, h)),
            pl.BlockSpec((hg * MLA_V, N_META), lambda b, h, i: (h, 0)),
        ] + cast_specs,
        out_specs=[pl.BlockSpec((tq, hg * MLA_V), lambda b, h, i: (b * nq + i, h))] + cast_specs,
        out_shape=[jax.ShapeDtypeStruct((batch * seq, D_MLA), bf16)] + [
            jax.ShapeDtypeStruct(w.shape, bf16) for w in weights_to_cast],
        scratch_shapes=[
            pltpu.VMEM((2, hg, tq, tq), f32),
            pltpu.VMEM((hg, 1, tq), f32),
            pltpu.VMEM((hg, 1, tq), f32),
            pltpu.VMEM((hg, MLA_V, tq), f32),
        ],
        compiler_params=pltpu.CompilerParams(
            dimension_semantics=("parallel", "parallel", "arbitrary"),
            vmem_limit_bytes=VMEM_LIMIT),
        name="mla_attention",
    )(qt, kc, vt, kc_meta, vt_meta, *weights_to_cast)


def _mix_out_body(ret_ref, mla_ref, wo_ref, h_ref, post_ref, npre_ref, o_ref, xo_ref, *, ts):
    for r in range(h_ref.shape[0] // ts):
        rows = slice(r * ts, (r + 1) * ts)
        m = (_dot(ret_ref[rows, :], wo_ref[:D_RET, :]) +
             _dot(mla_ref[rows, :], wo_ref[D_RET:, :]))
        hn = h_ref[rows, :] + _rms(m, post_ref[...])
        o_ref[rows, :] = hn
        xo_ref[rows, :] = _rms(hn, npre_ref[...]).astype(bf16)


def _mix_out(ret, mla, wo, h, post_w, next_pre_w, *, tm, ts):
    n = h.shape[0]
    row = pl.BlockSpec((tm, D_MODEL), lambda i: (i, 0))
    return pl.pallas_call(
        functools.partial(_mix_out_body, ts=ts),
        grid=(n // tm,),
        in_specs=[
            pl.BlockSpec((tm, D_RET), lambda i: (i, 0)),
            pl.BlockSpec((tm, D_MLA), lambda i: (i, 0)),
            _resident(wo.shape),
            row,
            _resident((1, D_MODEL)),
            _resident((1, D_MODEL)),
        ],
        out_specs=[row, row],
        out_shape=[jax.ShapeDtypeStruct((n, D_MODEL), f32),
                   jax.ShapeDtypeStruct((n, D_MODEL), bf16)],
        compiler_params=pltpu.CompilerParams(
            dimension_semantics=("parallel",), vmem_limit_bytes=VMEM_LIMIT),
        name="mix_out",
    )(ret, mla, wo, h, post_w, next_pre_w)


def _rope_tables(seq_len):
    pos = np.arange(seq_len, dtype=np.float64)

    def cos_sin(dim):
        inv = ROPE_THETA ** (-np.arange(0, dim, 2, dtype=np.float64) / dim)
        ang = pos[:, None] * inv[None, :]
        return np.cos(ang), np.sin(ang)

    c, s = cos_sin(RET_DIM)
    cr = np.concatenate([c, c], axis=-1)
    sr = np.concatenate([-s, s], axis=-1)
    c, s = cos_sin(MLA_ROPE)
    z = np.zeros_like(c)
    cm = np.concatenate([c, z, c, z], axis=-1)
    sm = np.concatenate([-s, z, s, z], axis=-1)
    return tuple(t.astype(np.float32) for t in (cr, sr, cm, sm))


def _pad_rope_cols(w):
    half = MLA_ROPE // 2
    z = jnp.zeros(w.shape[:-1] + (half,), w.dtype)
    return jnp.concatenate([w[..., :half], z, w[..., half:], z], axis=-1)


def kernel(x, meta_tokens, ffn1_pre_norm, ffn1_w_gate, ffn1_w_up, ffn1_w_down, ffn1_post_norm, mix_pre_norm, w_in, ret_group_norm, mla_q_norm, mla_w_uq, mla_kv_norm, mla_w_uk, mla_w_uv, w_out, mix_post_norm, ffn2_pre_norm, ffn2_w_gate, ffn2_w_up, ffn2_w_down, ffn2_post_norm):
    batch, seq, _ = x.shape
    assert ffn1_pre_norm.shape[0] == 1, "single-layer trunk only"
    l = 0
    cr, sr, cm, sm = _rope_tables(N_META + seq)
    tabs_meta = tuple(t[:N_META] for t in (cr, sr, cm, sm)) + (cm[:N_META].T, sm[:N_META].T)
    tabs_real = tuple(t[N_META:] for t in (cr, sr, cm, sm)) + (cm[N_META:].T, sm[N_META:].T)
    log_g = jnp.log(1.0 - 2.0 ** (-5.0 - jnp.arange(RET_HEADS, dtype=f32)))
    log_g = jnp.broadcast_to(log_g[:, None, None], (RET_HEADS, 1, LANES))

    h = x.reshape(batch * seq, D_MODEL)
    hm = meta_tokens.astype(x.dtype)
    vec = lambda w: w[l].reshape(1, -1)
    tf = 512
    ffn1 = (vec(ffn1_pre_norm), ffn1_w_gate[l].astype(bf16), ffn1_w_up[l].astype(bf16),
            ffn1_w_down[l].astype(bf16), vec(ffn1_post_norm))
    win = w_in[l].astype(bf16)
    wkr = _pad_rope_cols(w_in[l][:, _OFF_KR:]).astype(bf16)
    wuq = mla_w_uq[l].reshape(MLA_Q_RANK, MLA_HEADS, MLA_NOPE + MLA_ROPE)
    wuq = jnp.concatenate([wuq[..., :MLA_NOPE], _pad_rope_cols(wuq[..., MLA_NOPE:])], axis=-1)
    wuqt = wuq.reshape(MLA_Q_RANK, MLA_HEADS * MLA_QK_PAD).T.astype(bf16)
    mix_w = (win, wkr, vec(mla_q_norm), wuqt, vec(mla_kv_norm),
             mla_w_uk[l].astype(bf16), mla_w_uv[l].T.astype(bf16))

    h, xn = _ffn(h, *ffn1, vec(mix_pre_norm), tm=1024, ts=256, tf=tf,
                 vmem_limit=VMEM_LIMIT_WIDE)
    _, xn_m = _ffn(hm, *ffn1, vec(mix_pre_norm), tm=N_META, ts=N_META, tf=tf)
    _, rk_m, rv_m, _, _, kc_m, vt_m = _mix_in(xn_m, *mix_w, *tabs_meta, tm=N_META, ts=N_META)
    rq, rk, rv, rg, qt, kc, vt = _mix_in(xn, *mix_w, *tabs_real, tm=512, ts=256)
    ret = _retention(log_g, rq, rk, rv, rg, vec(ret_group_norm), rk_m, rv_m,
                     batch=batch, seq=seq, rows=512)
    later_weights = (w_out[l], ffn2_w_gate[l], ffn2_w_up[l], ffn2_w_down[l])
    mla, wo, wg2, wu2, wd2 = _mla(qt, kc, vt, kc_m, vt_m[0], later_weights,
                                  batch=batch, seq=seq, hg=4)
    h, xn = _mix_out(ret, mla, wo, h, vec(mix_post_norm), vec(ffn2_pre_norm), tm=512, ts=128)
    h = _ffn(h, xn, wg2, wu2, wd2, vec(ffn2_post_norm), tm=1024, ts=256, tf=tf)
    return h.reshape(batch, seq, D_MODEL)
```

```python
import functools

import jax
import jax.numpy as jnp
import numpy as np
from jax import lax
from jax.experimental import pallas as pl
from jax.experimental.pallas import tpu as pltpu

D_MODEL = 2048
N_META = 16
CHUNK = 128
RET_HEADS = 8
RET_DIM = 128
MLA_HEADS = 8
MLA_NOPE = 128
MLA_ROPE = 64
MLA_V = 128
MLA_Q_RANK = 512
MLA_KV_RANK = 256
MLA_QK_PAD = 256
D_RET = RET_HEADS * RET_DIM
D_MLA = MLA_HEADS * MLA_V
D_FF = 5632
ROPE_THETA = 10000.0
EPS = 1e-6
LANES = 128

_OFF_RQ = 0
_OFF_RK = D_RET
_OFF_RV = 2 * D_RET
_OFF_RG = 3 * D_RET
_OFF_CQ = 4 * D_RET
_OFF_CKV = _OFF_CQ + MLA_Q_RANK
_OFF_KR = _OFF_CKV + MLA_KV_RANK

MLA_Q_SCALE = (MLA_NOPE + MLA_ROPE) ** -0.5 * 1.4426950408889634

VMEM_LIMIT = 60 * 1024 * 1024
VMEM_LIMIT_WIDE = 63 * 1024 * 1024

f32 = jnp.float32
bf16 = jnp.bfloat16


def _rms(x, w):
    return x * lax.rsqrt(jnp.mean(x * x, axis=-1, keepdims=True) + EPS) * w


def _dot(a, b):
    return jnp.dot(a, b, preferred_element_type=f32)


def _dot_nt(a, b):
    return lax.dot_general(a, b, (((1,), (1,)), ((), ())), preferred_element_type=f32)


def _dot_tn(a, b):
    return lax.dot_general(a, b, (((0,), (0,)), ((), ())), preferred_element_type=f32)


def _resident(shape):
    nd = len(shape)
    return pl.BlockSpec(shape, lambda *_: (0,) * nd, pipeline_mode=pl.Buffered(1))


def _ffn_body(*refs, ts, norm_input, emit_next):
    refs = list(refs)
    h_ref = refs.pop(0)
    pre_ref = refs.pop(0)
    wg_ref, wu_ref, wd_ref, post_ref = (refs.pop(0) for _ in range(4))
    npre_ref = refs.pop(0) if emit_next else None
    o_ref = refs.pop(0)
    xo_ref = refs.pop(0) if emit_next else None
    xn_ref = refs.pop(0) if norm_input else pre_ref
    j = pl.program_id(1)
    last = pl.num_programs(1) - 1

    def run(first, final):
        step_rows = ts if (first or final) else min(2 * ts, h_ref.shape[0])
        for r in range(h_ref.shape[0] // step_rows):
            rows = slice(r * step_rows, (r + 1) * step_rows)
            if norm_input and first:
                xn = _rms(h_ref[rows, :], pre_ref[...]).astype(bf16)
                xn_ref[rows, :] = xn
            else:
                xn = xn_ref[rows, :]
            g = _dot(xn, wg_ref[...])
            u = _dot(xn, wu_ref[...])
            a = (g * jax.nn.sigmoid(g) * u).astype(bf16)
            acc = _dot(a, wd_ref[...])
            if not first:
                acc += o_ref[rows, :]
            if final:
                acc = h_ref[rows, :] + 0.5 * _rms(acc, post_ref[...])
                if emit_next:
                    xo_ref[rows, :] = _rms(acc, npre_ref[...]).astype(bf16)
            o_ref[rows, :] = acc

    pl.when(j == 0)(lambda: run(True, False))
    pl.when(jnp.logical_and(j > 0, j < last))(lambda: run(False, False))
    pl.when(j == last)(lambda: run(False, True))


def _ffn(h, pre, wg, wu, wd, post_w, next_pre_w=None, *, tm, ts, tf, vmem_limit=None):
    vmem_limit = VMEM_LIMIT if vmem_limit is None else vmem_limit
    n = h.shape[0]
    norm_input = pre.shape[0] == 1
    emit_next = next_pre_w is not None
    assert D_FF // tf >= 2 and tm % ts == 0
    row = pl.BlockSpec((tm, D_MODEL), lambda i, j: (i, 0))
    vec = pl.BlockSpec((1, D_MODEL), lambda i, j: (0, 0))
    in_specs = [
        row,
        vec if norm_input else row,
        pl.BlockSpec((D_MODEL, tf), lambda i, j: (0, j)),
        pl.BlockSpec((D_MODEL, tf), lambda i, j: (0, j)),
        pl.BlockSpec((tf, D_MODEL), lambda i, j: (j, 0)),
        vec,
    ]
    args = [h, pre, wg, wu, wd, post_w]
    out_specs, out_shape = [row], [jax.ShapeDtypeStruct((n, D_MODEL), f32)]
    if emit_next:
        in_specs.append(vec)
        args.append(next_pre_w)
        out_specs.append(row)
        out_shape.append(jax.ShapeDtypeStruct((n, D_MODEL), bf16))
    out = pl.pallas_call(
        functools.partial(_ffn_body, ts=ts, norm_input=norm_input, emit_next=emit_next),
        grid=(n // tm, D_FF // tf),
        in_specs=in_specs,
        out_specs=out_specs,
        out_shape=out_shape,
        scratch_shapes=[pltpu.VMEM((tm, D_MODEL), bf16)] if norm_input else [],
        compiler_params=pltpu.CompilerParams(
            dimension_semantics=("parallel", "arbitrary"), vmem_limit_bytes=vmem_limit),
        name="swiglu",
    )(*args)
    return out if emit_next else out[0]


def _rot(x, c, s):
    return x * c + pltpu.roll(x, 64, 1) * s


def _mix_in_body(x_ref, win_ref, wkr_ref, qn_ref, wuqt_ref, kvn_ref, wuk_ref, wuvt_ref,
                 cr_ref, sr_ref, cm_ref, sm_ref, cmt_ref, smt_ref,
                 rq_ref, rk_ref, rv_ref, rg_ref, qt_ref, kc_ref, vt_ref, *, ts):
    k_scale = RET_DIM ** -0.5
    pair = 2 * RET_DIM
    half = LANES // 2
    for r in range(x_ref.shape[0] // ts):
        rows = slice(r * ts, (r + 1) * ts)
        xn = x_ref[rows, :]
        cr, sr = cr_ref[rows, :], sr_ref[rows, :]
        cm, sm = cm_ref[rows, :], sm_ref[rows, :]

        def proj(lo, width):
            return _dot(xn, win_ref[:, lo:lo + width])

        for lo in range(0, D_RET, pair):
            pq, pk = proj(_OFF_RQ + lo, pair), proj(_OFF_RK + lo, pair)
            for off in (0, RET_DIM):
                sl = slice(lo + off, lo + off + RET_DIM)
                rq_ref[rows, sl] = _rot(pq[:, off:off + RET_DIM], cr, sr).astype(bf16)
                rk_ref[rows, sl] = (
                    _rot(pk[:, off:off + RET_DIM], cr, sr) * k_scale).astype(bf16)
        for lo in range(0, D_RET, pair):
            rv_ref[rows, lo:lo + pair] = proj(_OFF_RV + lo, pair).astype(bf16)
            rg_ref[rows, lo:lo + pair] = proj(_OFF_RG + lo, pair)

        cq = _rms(proj(_OFF_CQ, MLA_Q_RANK), qn_ref[...]).astype(bf16)
        ckv = _rms(proj(_OFF_CKV, MLA_KV_RANK), kvn_ref[...]).astype(bf16)
        kr = _rot(_dot(xn, wkr_ref[...]), cm, sm).astype(bf16)
        cmt, smt = cmt_ref[:, rows], smt_ref[:, rows]
        for hd in range(MLA_HEADS):
            lo = hd * MLA_QK_PAD
            qt = _dot_nt(wuqt_ref[lo:lo + MLA_QK_PAD, :], cq)
            qr = qt[MLA_NOPE:]
            qr = qr * cmt + jnp.concatenate([qr[half:], qr[:half]], axis=0) * smt
            qt_ref[lo:lo + MLA_NOPE, rows] = (qt[:MLA_NOPE] * MLA_Q_SCALE).astype(bf16)
            qt_ref[lo + MLA_NOPE:lo + MLA_QK_PAD, rows] = (qr * MLA_Q_SCALE).astype(bf16)
            kc_ref[rows, lo + MLA_NOPE:lo + MLA_QK_PAD] = kr
        for lo in range(0, D_MLA, 2 * MLA_V):
            kn = _dot(ckv, wuk_ref[:, lo:lo + 2 * MLA_V])
            for off in (0, MLA_NOPE):
                hd = (lo + off) // MLA_NOPE
                kc_ref[rows, hd * MLA_QK_PAD:hd * MLA_QK_PAD + MLA_NOPE] = (
                    kn[:, off:off + MLA_NOPE].astype(bf16))
            vt_ref[0, lo:lo + 2 * MLA_V, rows] = _dot_nt(wuvt_ref[lo:lo + 2 * MLA_V, :],
                                                         ckv).astype(bf16)


def _mix_in(xn, win, wkr, qn_w, wuqt, kvn_w, wuk, wuvt, cr, sr, cm, sm, cmt, smt, *, tm, ts):
    n = xn.shape[0]
    n_pos_blocks = cr.shape[0] // tm

    def row(width):
        return pl.BlockSpec((tm, width), lambda i: (i, 0))

    tab = pl.BlockSpec((tm, LANES), lambda i: (i % n_pos_blocks, 0))
    tab_t = pl.BlockSpec((LANES, tm), lambda i: (0, i % n_pos_blocks))
    d_qk = MLA_HEADS * MLA_QK_PAD
    out_shape = [
        jax.ShapeDtypeStruct((n, D_RET), bf16),
        jax.ShapeDtypeStruct((n, D_RET), bf16),
        jax.ShapeDtypeStruct((n, D_RET), bf16),
        jax.ShapeDtypeStruct((n, D_RET), f32),
        jax.ShapeDtypeStruct((d_qk, n), bf16),
        jax.ShapeDtypeStruct((n, d_qk), bf16),
        jax.ShapeDtypeStruct((n // tm, D_MLA, tm), bf16),
    ]
    return pl.pallas_call(
        functools.partial(_mix_in_body, ts=ts),
        grid=(n // tm,),
        in_specs=[
            row(D_MODEL),
            _resident(win.shape),
            _resident(wkr.shape),
            _resident((1, MLA_Q_RANK)),
            _resident(wuqt.shape),
            _resident((1, MLA_KV_RANK)),
            _resident(wuk.shape),
            _resident(wuvt.shape),
            tab, tab, tab, tab, tab_t, tab_t,
        ],
        out_specs=[row(D_RET), row(D_RET), row(D_RET), row(D_RET),
                   pl.BlockSpec((d_qk, tm), lambda i: (0, i)),
                   row(d_qk),
                   pl.BlockSpec((1, D_MLA, tm), lambda i: (i, 0, 0))],
        out_shape=out_shape,
        compiler_params=pltpu.CompilerParams(
            dimension_semantics=("parallel",), vmem_limit_bytes=VMEM_LIMIT),
        name="mix_in",
    )(xn, win, wkr, qn_w, wuqt, kvn_w, wuk, wuvt, cr, sr, cm, sm, cmt, smt)


def _retention_body(lg_ref, rq_ref, rk_ref, rv_ref, rg_ref, gn_ref, km_ref, vm_ref, o_ref,
                    dmask_ref, xi_ref, zeta_ref, state_ref, *, rows):
    t = pl.program_id(1)

    @pl.when(t == 0)
    def _():
        row = lax.broadcasted_iota(jnp.int32, (CHUNK, CHUNK), 0).astype(f32)
        col = lax.broadcasted_iota(jnp.int32, (CHUNK, CHUNK), 1).astype(f32)
        diff = row - col
        mrow = lax.broadcasted_iota(jnp.int32, (N_META, RET_DIM), 0).astype(f32)
        for hd in range(RET_HEADS):
            lg = lg_ref[hd]
            sl = slice(hd * RET_DIM, (hd + 1) * RET_DIM)
            dmask_ref[hd] = jnp.where(diff >= 0, jnp.exp(jnp.maximum(diff, 0.0) * lg), 0.0)
            xi_ref[hd] = jnp.exp((row + 1.0) * lg)
            zeta_ref[hd] = jnp.exp((CHUNK - 1.0 - row) * lg)
            zeta_m = jnp.exp((N_META - 1.0 - mrow) * lg)
            state_ref[hd] = _dot_tn((km_ref[:, sl].astype(f32) * zeta_m).astype(bf16),
                                    vm_ref[:, sl])

    chunks = [slice(c * CHUNK, (c + 1) * CHUNK) for c in range(rows // CHUNK)]

    def head_cols(hd):
        return slice(hd * RET_DIM, (hd + 1) * RET_DIM)

    def state_free(hd, r):
        sl = head_cols(hd)
        q, k, v = rq_ref[r, sl], rk_ref[r, sl], rv_ref[r, sl]
        s = (_dot_nt(q, k) * dmask_ref[hd]).astype(bf16)
        inc = _dot_tn((k.astype(f32) * zeta_ref[hd]).astype(bf16), v)
        return s, inc

    def state_bound(hd, r, s, inc, state):
        sl = head_cols(hd)
        o = _dot(s, rv_ref[r, sl]) + _dot(rq_ref[r, sl], state.astype(bf16)) * xi_ref[hd]
        g = rg_ref[r, sl]
        o_ref[r, sl] = (_rms(o, gn_ref[:, sl]) * (g * jax.nn.sigmoid(g))).astype(bf16)
        return state * jnp.exp(CHUNK * lg_ref[hd]) + inc

    items = [(hd, r) for hd in range(RET_HEADS) for r in chunks]
    ahead = 2
    staged = {n: state_free(*items[n]) for n in range(ahead)}
    state = None
    for n, (hd, r) in enumerate(items):
        if n + ahead < len(items):
            staged[n + ahead] = state_free(*items[n + ahead])
        if r is chunks[0]:
            state = state_ref[hd]
        state = state_bound(hd, r, *staged.pop(n), state)
        if r is chunks[-1]:
            state_ref[hd] = state


def _retention(log_g, rq, rk, rv, rg, gn_w, rk_meta, rv_meta, *, batch, seq, rows):
    steps = seq // rows
    blk = pl.BlockSpec((rows, D_RET), lambda b, t: (b * steps + t, 0))
    head_tab = pltpu.VMEM((RET_HEADS, CHUNK, CHUNK), f32)
    return pl.pallas_call(
        functools.partial(_retention_body, rows=rows),
        grid=(batch, steps),
        in_specs=[
            _resident((RET_HEADS, 1, LANES)),
            blk, blk, blk, blk,
            _resident((1, D_RET)),
            _resident((N_META, D_RET)),
            _resident((N_META, D_RET)),
        ],
        out_specs=blk,
        out_shape=jax.ShapeDtypeStruct((batch * seq, D_RET), bf16),
        scratch_shapes=[head_tab, head_tab, head_tab,
                        pltpu.VMEM((RET_HEADS, RET_DIM, RET_DIM), f32)],
        compiler_params=pltpu.CompilerParams(
            dimension_semantics=("parallel", "arbitrary"), vmem_limit_bytes=VMEM_LIMIT),
        name="retention",
    )(log_g, rq, rk, rv, rg, gn_w, rk_meta, rv_meta)


def _mla_body(qt_ref, k_ref, vt_ref, km_ref, vmt_ref, *rest, tq, hg, n_casts):
    cast_src, o_ref, cast_dst = rest[:n_casts], rest[n_casts], rest[n_casts + 1:2 * n_casts + 1]
    s_ref, m_ref, l_ref, acc_ref = rest[2 * n_casts + 1:]
    qi = pl.program_id(2)
    heads = range(hg)

    def qk_cols(g):
        return slice(g * MLA_QK_PAD, (g + 1) * MLA_QK_PAD)

    def v_cols(g):
        return slice(g * MLA_V, (g + 1) * MLA_V)

    def side_casts():
        for src, dst in zip(cast_src, cast_dst):
            dst[...] = src[...].astype(bf16)

    def meta_scores():
        return [_dot(km_ref[:, qk_cols(g)], qt_ref[qk_cols(g), :]) for g in heads]

    def meta_keys(s_all):
        for g in heads:
            s = s_all[g]
            m0 = jnp.max(s, axis=0, keepdims=True)
            p = jnp.exp2(s - m0)
            m_ref[g] = m0
            l_ref[g] = jnp.sum(p, axis=0, keepdims=True)
            acc_ref[g] = _dot(vmt_ref[v_cols(g), :], p.astype(bf16))

    def head_scores(j, slot, g):
        r = pl.ds(pl.multiple_of(j * tq, tq), tq)
        s_ref[slot, g] = _dot(k_ref[r, qk_cols(g)], qt_ref[qk_cols(g), :])

    def scores(j, slot):
        for g in heads:
            head_scores(j, slot, g)

    def head_block(j, slot, g, masked):
        s = s_ref[slot, g]
        if masked:
            kpos = lax.broadcasted_iota(jnp.int32, (tq, tq), 0)
            qpos = lax.broadcasted_iota(jnp.int32, (tq, tq), 1)
            s = jnp.where(kpos <= qpos, s, -1e30)
        m_old = m_ref[g]
        m_new = jnp.maximum(m_old, jnp.max(s, axis=0, keepdims=True))
        alpha = jnp.exp2(m_old - m_new)
        p = jnp.exp2(s - m_new)
        l_ref[g] = alpha * l_ref[g] + jnp.sum(p, axis=0, keepdims=True)
        acc_ref[g] = alpha * acc_ref[g] + _dot(vt_ref[j, v_cols(g), :], p.astype(bf16))
        m_ref[g] = m_new

    def block(j, slot, masked):
        for g in heads:
            head_block(j, slot, g, masked)

    def step(j, slot):
        lead = min(2, hg)
        for g in range(lead):
            head_scores(j + 1, 1 - slot, g)
        for g in heads:
            head_block(j, slot, g, masked=False)
            if g + lead < hg:
                head_scores(j + 1, 1 - slot, g + lead)

    def pair(i, carry):
        step(2 * i, 0)
        step(2 * i + 1, 1)
        return carry

    s_meta = meta_scores()
    scores(0, 0)
    meta_keys(s_meta)
    lax.fori_loop(0, lax.shift_right_logical(qi, 1), pair, 0)
    odd = (qi & 1) == 1

    @pl.when(odd)
    def _():
        side_casts()
        step(qi - 1, 0)
        block(qi, 1, masked=True)

    @pl.when(jnp.logical_not(odd))
    def _():
        side_casts()
        block(qi, 0, masked=True)

    for g in heads:
        o_ref[:, v_cols(g)] = (acc_ref[g] / l_ref[g]).T.astype(bf16)


def _cast_block_spec(shape, grid):
    steps = int(np.prod(grid))
    rows, cols = shape
    for col_parts in (1, 2, 4, 8, 16):
        row_parts = steps // col_parts
        if (steps % col_parts == 0 and rows % (16 * row_parts) == 0
                and cols % (LANES * col_parts) == 0):
            break
    else:
        raise ValueError(f"no per-step blocking of {shape} over {steps} grid steps")

    def index_map(*idx):
        step = 0
        for i, n in zip(idx, grid):
            step = step * n + i
        return step // col_parts, step % col_parts

    return pl.BlockSpec((rows // row_parts, cols // col_parts), index_map)


def _mla(qt, kc, vt, kc_meta, vt_meta, weights_to_cast=(), *, batch, seq, hg):
    tq = vt.shape[-1]
    nq = seq // tq
    grid = (batch, MLA_HEADS // hg, nq)
    cast_specs = [_cast_block_spec(w.shape, grid) for w in weights_to_cast]
    return pl.pallas_call(
        functools.partial(_mla_body, tq=tq, hg=hg, n_casts=len(cast_specs)),
        grid=grid,
        in_specs=[
            pl.BlockSpec((hg * MLA_QK_PAD, tq), lambda b, h, i: (h, b * nq + i)),
            pl.BlockSpec((seq, hg * MLA_QK_PAD), lambda b, h, i: (b, h)),
            pl.BlockSpec((nq, hg * MLA_V, tq), lambda b, h, i: (b, h, 0)),
            pl.BlockSpec((N_META, hg * MLA_QK_PAD), lambda b, h, i: (0, h)),
            pl.BlockSpec((hg * MLA_V, N_META), lambda b, h, i: (h, 0)),
        ] + cast_specs,
        out_specs=[pl.BlockSpec((tq, hg * MLA_V), lambda b, h, i: (b * nq + i, h))] + cast_specs,
        out_shape=[jax.ShapeDtypeStruct((batch * seq, D_MLA), bf16)] + [
            jax.ShapeDtypeStruct(w.shape, bf16) for w in weights_to_cast],
        scratch_shapes=[
            pltpu.VMEM((2, hg, tq, tq), f32),
            pltpu.VMEM((hg, 1, tq), f32),
            pltpu.VMEM((hg, 1, tq), f32),
            pltpu.VMEM((hg, MLA_V, tq), f32),
        ],
        compiler_params=pltpu.CompilerParams(
            dimension_semantics=("parallel", "parallel", "arbitrary"),
            vmem_limit_bytes=VMEM_LIMIT),
        name="mla_attention",
    )(qt, kc, vt, kc_meta, vt_meta, *weights_to_cast)


def _mix_out_body(ret_ref, mla_ref, wo_ref, h_ref, post_ref, npre_ref, o_ref, xo_ref, *, ts):
    for r in range(h_ref.shape[0] // ts):
        rows = slice(r * ts, (r + 1) * ts)
        mixed = jnp.concatenate([ret_ref[rows, :], mla_ref[rows, :]], axis=1)
        m = _dot(mixed, wo_ref[...])
        hn = h_ref[rows, :] + _rms(m, post_ref[...])
        o_ref[rows, :] = hn
        xo_ref[rows, :] = _rms(hn, npre_ref[...]).astype(bf16)


def _mix_out(ret, mla, wo, h, post_w, next_pre_w, *, tm, ts):
    n = h.shape[0]
    row = pl.BlockSpec((tm, D_MODEL), lambda i: (i, 0))
    return pl.pallas_call(
        functools.partial(_mix_out_body, ts=ts),
        grid=(n // tm,),
        in_specs=[
            pl.BlockSpec((tm, D_RET), lambda i: (i, 0)),
            pl.BlockSpec((tm, D_MLA), lambda i: (i, 0)),
            _resident(wo.shape),
            row,
            _resident((1, D_MODEL)),
            _resident((1, D_MODEL)),
        ],
        out_specs=[row, row],
        out_shape=[jax.ShapeDtypeStruct((n, D_MODEL), f32),
                   jax.ShapeDtypeStruct((n, D_MODEL), bf16)],
        compiler_params=pltpu.CompilerParams(
            dimension_semantics=("parallel",), vmem_limit_bytes=VMEM_LIMIT),
        name="mix_out",
    )(ret, mla, wo, h, post_w, next_pre_w)


def _rope_tables(seq_len):
    pos = np.arange(seq_len, dtype=np.float64)

    def cos_sin(dim):
        inv = ROPE_THETA ** (-np.arange(0, dim, 2, dtype=np.float64) / dim)
        ang = pos[:, None] * inv[None, :]
        return np.cos(ang), np.sin(ang)

    c, s = cos_sin(RET_DIM)
    cr = np.concatenate([c, c], axis=-1)
    sr = np.concatenate([-s, s], axis=-1)
    c, s = cos_sin(MLA_ROPE)
    z = np.zeros_like(c)
    cm = np.concatenate([c, z, c, z], axis=-1)
    sm = np.concatenate([-s, z, s, z], axis=-1)
    return tuple(t.astype(np.float32) for t in (cr, sr, cm, sm))


def _pad_rope_cols(w):
    half = MLA_ROPE // 2
    z = jnp.zeros(w.shape[:-1] + (half,), w.dtype)
    return jnp.concatenate([w[..., :half], z, w[..., half:], z], axis=-1)


def kernel(x, meta_tokens, ffn1_pre_norm, ffn1_w_gate, ffn1_w_up, ffn1_w_down, ffn1_post_norm, mix_pre_norm, w_in, ret_group_norm, mla_q_norm, mla_w_uq, mla_kv_norm, mla_w_uk, mla_w_uv, w_out, mix_post_norm, ffn2_pre_norm, ffn2_w_gate, ffn2_w_up, ffn2_w_down, ffn2_post_norm):
    batch, seq, _ = x.shape
    assert ffn1_pre_norm.shape[0] == 1, "single-layer trunk only"
    l = 0
    cr, sr, cm, sm = _rope_tables(N_META + seq)
    tabs_meta = tuple(t[:N_META] for t in (cr, sr, cm, sm)) + (cm[:N_META].T, sm[:N_META].T)
    tabs_real = tuple(t[N_META:] for t in (cr, sr, cm, sm)) + (cm[N_META:].T, sm[N_META:].T)
    log_g = jnp.log(1.0 - 2.0 ** (-5.0 - jnp.arange(RET_HEADS, dtype=f32)))
    log_g = jnp.broadcast_to(log_g[:, None, None], (RET_HEADS, 1, LANES))

    h = x.reshape(batch * seq, D_MODEL)
    hm = meta_tokens.astype(x.dtype)
    vec = lambda w: w[l].reshape(1, -1)
    tf = 512
    ffn1 = (vec(ffn1_pre_norm), ffn1_w_gate[l].astype(bf16), ffn1_w_up[l].astype(bf16),
            ffn1_w_down[l].astype(bf16), vec(ffn1_post_norm))
    win = w_in[l].astype(bf16)
    wkr = _pad_rope_cols(w_in[l][:, _OFF_KR:]).astype(bf16)
    wuq = mla_w_uq[l].reshape(MLA_Q_RANK, MLA_HEADS, MLA_NOPE + MLA_ROPE)
    wuq = jnp.concatenate([wuq[..., :MLA_NOPE], _pad_rope_cols(wuq[..., MLA_NOPE:])], axis=-1)
    wuqt = wuq.reshape(MLA_Q_RANK, MLA_HEADS * MLA_QK_PAD).T.astype(bf16)
    mix_w = (win, wkr, vec(mla_q_norm), wuqt, vec(mla_kv_norm),
             mla_w_uk[l].astype(bf16), mla_w_uv[l].T.astype(bf16))

    h, xn = _ffn(h, *ffn1, vec(mix_pre_norm), tm=1024, ts=256, tf=tf,
                 vmem_limit=VMEM_LIMIT_WIDE)
    _, xn_m = _ffn(hm, *ffn1, vec(mix_pre_norm), tm=N_META, ts=N_META, tf=tf)
    _, rk_m, rv_m, _, _, kc_m, vt_m = _mix_in(xn_m, *mix_w, *tabs_meta, tm=N_META, ts=N_META)
    rq, rk, rv, rg, qt, kc, vt = _mix_in(xn, *mix_w, *tabs_real, tm=512, ts=256)
    ret = _retention(log_g, rq, rk, rv, rg, vec(ret_group_norm), rk_m, rv_m,
                     batch=batch, seq=seq, rows=512)
    later_weights = (w_out[l], ffn2_w_gate[l], ffn2_w_up[l], ffn2_w_down[l])
    mla, wo, wg2, wu2, wd2 = _mla(qt, kc, vt, kc_m, vt_m[0], later_weights,
                                  batch=batch, seq=seq, hg=4)
    h, xn = _mix_out(ret, mla, wo, h, vec(mix_post_norm), vec(ffn2_pre_norm), tm=512, ts=128)
    h = _ffn(h, xn, wg2, wu2, wd2, vec(ffn2_post_norm), tm=1024, ts=256, tf=tf)
    return h.reshape(batch, seq, D_MODEL)
```

```python
import functools

import jax
import jax.numpy as jnp
import numpy as np
from jax import lax
from jax.experimental import pallas as pl
from jax.experimental.pallas import tpu as pltpu

D_MODEL = 2048
N_META = 16
CHUNK = 128
RET_HEADS = 8
RET_DIM = 128
MLA_HEADS = 8
MLA_NOPE = 128
MLA_ROPE = 64
MLA_V = 128
MLA_Q_RANK = 512
MLA_KV_RANK = 256
MLA_QK_PAD = 256
D_RET = RET_HEADS * RET_DIM
D_MLA = MLA_HEADS * MLA_V
D_FF = 5632
ROPE_THETA = 10000.0
EPS = 1e-6
LANES = 128

_OFF_RQ = 0
_OFF_RK = D_RET
_OFF_RV = 2 * D_RET
_OFF_RG = 3 * D_RET
_OFF_CQ = 4 * D_RET
_OFF_CKV = _OFF_CQ + MLA_Q_RANK
_OFF_KR = _OFF_CKV + MLA_KV_RANK

MLA_Q_SCALE = (MLA_NOPE + MLA_ROPE) ** -0.5 * 1.4426950408889634

VMEM_LIMIT = 60 * 1024 * 1024
VMEM_LIMIT_WIDE = 64 * 1024 * 1024

f32 = jnp.float32
bf16 = jnp.bfloat16


def _rms(x, w):
    return x * lax.rsqrt(jnp.mean(x * x, axis=-1, keepdims=True) + EPS) * w


def _dot(a, b):
    return jnp.dot(a, b, preferred_element_type=f32)


def _dot_nt(a, b):
    return lax.dot_general(a, b, (((1,), (1,)), ((), ())), preferred_element_type=f32)


def _dot_tn(a, b):
    return lax.dot_general(a, b, (((0,), (0,)), ((), ())), preferred_element_type=f32)


def _resident(shape):
    nd = len(shape)
    return pl.BlockSpec(shape, lambda *_: (0,) * nd, pipeline_mode=pl.Buffered(1))


def _ffn_body(*refs, ts, norm_input, emit_next, host_cast):
    refs = list(refs)
    h_ref = refs.pop(0)
    pre_ref = refs.pop(0)
    wg_ref, wu_ref, wd_ref, post_ref = (refs.pop(0) for _ in range(4))
    npre_ref = refs.pop(0) if emit_next else None
    cast_src = refs.pop(0) if host_cast else None
    o_ref = refs.pop(0)
    xo_ref = refs.pop(0) if emit_next else None
    cast_dst = refs.pop(0) if host_cast else None
    xn_ref = refs.pop(0) if norm_input else pre_ref
    j = pl.program_id(1)
    last = pl.num_programs(1) - 1

    def run(first, final):
        step_rows = ts if (first or final) else min(2 * ts, h_ref.shape[0])
        if host_cast:
            cast_dst[...] = cast_src[...].astype(bf16)
        for r in range(h_ref.shape[0] // step_rows):
            rows = slice(r * step_rows, (r + 1) * step_rows)
            if norm_input and first:
                xn = _rms(h_ref[rows, :], pre_ref[...]).astype(bf16)
                xn_ref[rows, :] = xn
            else:
                xn = xn_ref[rows, :]
            g = _dot(xn, wg_ref[...])
            u = _dot(xn, wu_ref[...])
            a = (g * jax.nn.sigmoid(g) * u).astype(bf16)
            acc = _dot(a, wd_ref[...])
            if not first:
                acc += o_ref[rows, :]
            if final:
                acc = h_ref[rows, :] + 0.5 * _rms(acc, post_ref[...])
                if emit_next:
                    xo_ref[rows, :] = _rms(acc, npre_ref[...]).astype(bf16)
            o_ref[rows, :] = acc

    pl.when(j == 0)(lambda: run(True, False))
    pl.when(jnp.logical_and(j > 0, j < last))(lambda: run(False, False))
    pl.when(j == last)(lambda: run(False, True))


def _ffn(h, pre, wg, wu, wd, post_w, next_pre_w=None, weight_to_cast=None, *, tm, ts, tf,
         vmem_limit=None):
    vmem_limit = VMEM_LIMIT if vmem_limit is None else vmem_limit
    n = h.shape[0]
    norm_input = pre.shape[0] == 1
    emit_next = next_pre_w is not None
    assert D_FF // tf >= 2 and tm % ts == 0
    row = pl.BlockSpec((tm, D_MODEL), lambda i, j: (i, 0))
    vec = pl.BlockSpec((1, D_MODEL), lambda i, j: (0, 0))
    in_specs = [
        row,
        vec if norm_input else row,
        pl.BlockSpec((D_MODEL, tf), lambda i, j: (0, j)),
        pl.BlockSpec((D_MODEL, tf), lambda i, j: (0, j)),
        pl.BlockSpec((tf, D_MODEL), lambda i, j: (j, 0)),
        vec,
    ]
    args = [h, pre, wg, wu, wd, post_w]
    out_specs, out_shape = [row], [jax.ShapeDtypeStruct((n, D_MODEL), f32)]
    if emit_next:
        in_specs.append(vec)
        args.append(next_pre_w)
        out_specs.append(row)
        out_shape.append(jax.ShapeDtypeStruct((n, D_MODEL), bf16))
    host_cast = weight_to_cast is not None
    if host_cast:
        w_rows, n_j = weight_to_cast.shape[0], D_FF // tf
        cast_rows = next(c for c in range(16, w_rows + 1, 16)
                         if w_rows % c == 0 and w_rows // c <= (n // tm) * n_j)
        n_blocks = w_rows // cast_rows
        spec = pl.BlockSpec((cast_rows, weight_to_cast.shape[1]),
                            lambda i, j: (jnp.minimum(i * n_j + j, n_blocks - 1), 0))
        in_specs.append(spec)
        args.append(weight_to_cast)
        out_specs.append(spec)
        out_shape.append(jax.ShapeDtypeStruct(weight_to_cast.shape, bf16))
    out = pl.pallas_call(
        functools.partial(_ffn_body, ts=ts, norm_input=norm_input, emit_next=emit_next,
                          host_cast=host_cast),
        grid=(n // tm, D_FF // tf),
        in_specs=in_specs,
        out_specs=out_specs,
        out_shape=out_shape,
        scratch_shapes=[pltpu.VMEM((tm, D_MODEL), bf16)] if norm_input else [],
        compiler_params=pltpu.CompilerParams(
            dimension_semantics=("parallel", "arbitrary"), vmem_limit_bytes=vmem_limit),
        name="swiglu",
    )(*args)
    return out if len(out) > 1 else out[0]


def _rot(x, c, s):
    return x * c + pltpu.roll(x, 64, 1) * s


def _mix_in_body(x_ref, win_ref, wkr_ref, qn_ref, wuqt_ref, kvn_ref, wuk_ref, wuvt_ref,
                 cr_ref, sr_ref, cm_ref, sm_ref, cmt_ref, smt_ref,
                 rq_ref, rk_ref, rv_ref, rg_ref, qt_ref, kc_ref, vt_ref, *, ts):
    k_scale = RET_DIM ** -0.5
    pair = 2 * RET_DIM
    half = LANES // 2
    for r in range(x_ref.shape[0] // ts):
        rows = slice(r * ts, (r + 1) * ts)
        xn = x_ref[rows, :]
        cr, sr = cr_ref[rows, :], sr_ref[rows, :]
        cm, sm = cm_ref[rows, :], sm_ref[rows, :]

        def proj(lo, width):
            return _dot(xn, win_ref[:, lo:lo + width])

        for lo in range(0, D_RET, pair):
            pq, pk = proj(_OFF_RQ + lo, pair), proj(_OFF_RK + lo, pair)
            for off in (0, RET_DIM):
                sl = slice(lo + off, lo + off + RET_DIM)
                rq_ref[rows, sl] = _rot(pq[:, off:off + RET_DIM], cr, sr).astype(bf16)
                rk_ref[rows, sl] = (
                    _rot(pk[:, off:off + RET_DIM], cr, sr) * k_scale).astype(bf16)
        for lo in range(0, D_RET, pair):
            rv_ref[rows, lo:lo + pair] = proj(_OFF_RV + lo, pair).astype(bf16)
            rg_ref[rows, lo:lo + pair] = proj(_OFF_RG + lo, pair)

        cq = _rms(proj(_OFF_CQ, MLA_Q_RANK), qn_ref[...]).astype(bf16)
        ckv = _rms(proj(_OFF_CKV, MLA_KV_RANK), kvn_ref[...]).astype(bf16)
        kr = _rot(_dot(xn, wkr_ref[...]), cm, sm).astype(bf16)
        cmt, smt = cmt_ref[:, rows], smt_ref[:, rows]
        for hd in range(MLA_HEADS):
            lo = hd * MLA_QK_PAD
            qt = _dot_nt(wuqt_ref[lo:lo + MLA_QK_PAD, :], cq)
            qr = qt[MLA_NOPE:]
            qr = qr * cmt + jnp.concatenate([qr[half:], qr[:half]], axis=0) * smt
            qt_ref[lo:lo + MLA_NOPE, rows] = (qt[:MLA_NOPE] * MLA_Q_SCALE).astype(bf16)
            qt_ref[lo + MLA_NOPE:lo + MLA_QK_PAD, rows] = (qr * MLA_Q_SCALE).astype(bf16)
            kc_ref[rows, lo + MLA_NOPE:lo + MLA_QK_PAD] = kr
        for lo in range(0, D_MLA, 2 * MLA_V):
            kn = _dot(ckv, wuk_ref[:, lo:lo + 2 * MLA_V])
            for off in (0, MLA_NOPE):
                hd = (lo + off) // MLA_NOPE
                kc_ref[rows, hd * MLA_QK_PAD:hd * MLA_QK_PAD + MLA_NOPE] = (
                    kn[:, off:off + MLA_NOPE].astype(bf16))
            vt_ref[0, lo:lo + 2 * MLA_V, rows] = _dot_nt(wuvt_ref[lo:lo + 2 * MLA_V, :],
                                                         ckv).astype(bf16)


def _mix_in(xn, win, wkr, qn_w, wuqt, kvn_w, wuk, wuvt, cr, sr, cm, sm, cmt, smt, *, tm, ts):
    n = xn.shape[0]
    n_pos_blocks = cr.shape[0] // tm

    def row(width):
        return pl.BlockSpec((tm, width), lambda i: (i, 0))

    tab = pl.BlockSpec((tm, LANES), lambda i: (i % n_pos_blocks, 0))
    tab_t = pl.BlockSpec((LANES, tm), lambda i: (0, i % n_pos_blocks))
    d_qk = MLA_HEADS * MLA_QK_PAD
    out_shape = [
        jax.ShapeDtypeStruct((n, D_RET), bf16),
        jax.ShapeDtypeStruct((n, D_RET), bf16),
        jax.ShapeDtypeStruct((n, D_RET), bf16),
        jax.ShapeDtypeStruct((n, D_RET), f32),
        jax.ShapeDtypeStruct((d_qk, n), bf16),
        jax.ShapeDtypeStruct((n, d_qk), bf16),
        jax.ShapeDtypeStruct((n // tm, D_MLA, tm), bf16),
    ]
    return pl.pallas_call(
        functools.partial(_mix_in_body, ts=ts),
        grid=(n // tm,),
        in_specs=[
            row(D_MODEL),
            _resident(win.shape),
            _resident(wkr.shape),
            _resident((1, MLA_Q_RANK)),
            _resident(wuqt.shape),
            _resident((1, MLA_KV_RANK)),
            _resident(wuk.shape),
            _resident(wuvt.shape),
            tab, tab, tab, tab, tab_t, tab_t,
        ],
        out_specs=[row(D_RET), row(D_RET), row(D_RET), row(D_RET),
                   pl.BlockSpec((d_qk, tm), lambda i: (0, i)),
                   row(d_qk),
                   pl.BlockSpec((1, D_MLA, tm), lambda i: (i, 0, 0))],
        out_shape=out_shape,
        compiler_params=pltpu.CompilerParams(
            dimension_semantics=("parallel",), vmem_limit_bytes=VMEM_LIMIT),
        name="mix_in",
    )(xn, win, wkr, qn_w, wuqt, kvn_w, wuk, wuvt, cr, sr, cm, sm, cmt, smt)


def _retention_body(lg_ref, rq_ref, rk_ref, rv_ref, rg_ref, gn_ref, km_ref, vm_ref, o_ref,
                    dmask_ref, xi_ref, zeta_ref, state_ref, *, rows):
    t = pl.program_id(1)

    @pl.when(t == 0)
    def _():
        row = lax.broadcasted_iota(jnp.int32, (CHUNK, CHUNK), 0).astype(f32)
        col = lax.broadcasted_iota(jnp.int32, (CHUNK, CHUNK), 1).astype(f32)
        diff = row - col
        mrow = lax.broadcasted_iota(jnp.int32, (N_META, RET_DIM), 0).astype(f32)
        for hd in range(RET_HEADS):
            lg = lg_ref[hd]
            sl = slice(hd * RET_DIM, (hd + 1) * RET_DIM)
            dmask_ref[hd] = jnp.where(diff >= 0, jnp.exp(jnp.maximum(diff, 0.0) * lg), 0.0)
            xi_ref[hd] = jnp.exp((row + 1.0) * lg)
            zeta_ref[hd] = jnp.exp((CHUNK - 1.0 - row) * lg)
            zeta_m = jnp.exp((N_META - 1.0 - mrow) * lg)
            state_ref[hd] = _dot_tn((km_ref[:, sl].astype(f32) * zeta_m).astype(bf16),
                                    vm_ref[:, sl])

    chunks = [slice(c * CHUNK, (c + 1) * CHUNK) for c in range(rows // CHUNK)]

    def head_cols(hd):
        return slice(hd * RET_DIM, (hd + 1) * RET_DIM)

    def state_free(hd, r):
        sl = head_cols(hd)
        q, k, v = rq_ref[r, sl], rk_ref[r, sl], rv_ref[r, sl]
        s = (_dot_nt(q, k) * dmask_ref[hd]).astype(bf16)
        inc = _dot_tn((k.astype(f32) * zeta_ref[hd]).astype(bf16), v)
        return s, inc

    def state_bound(hd, r, s, inc, state):
        sl = head_cols(hd)
        o = _dot(s, rv_ref[r, sl]) + _dot(rq_ref[r, sl], state.astype(bf16)) * xi_ref[hd]
        g = rg_ref[r, sl]
        o_ref[r, sl] = (_rms(o, gn_ref[:, sl]) * (g * jax.nn.sigmoid(g))).astype(bf16)
        return state * jnp.exp(CHUNK * lg_ref[hd]) + inc

    items = [(hd, r) for hd in range(RET_HEADS) for r in chunks]
    ahead = 2
    staged = {n: state_free(*items[n]) for n in range(ahead)}
    state = None
    for n, (hd, r) in enumerate(items):
        if n + ahead < len(items):
            staged[n + ahead] = state_free(*items[n + ahead])
        if r is chunks[0]:
            state = state_ref[hd]
        state = state_bound(hd, r, *staged.pop(n), state)
        if r is chunks[-1]:
            state_ref[hd] = state


def _retention(log_g, rq, rk, rv, rg, gn_w, rk_meta, rv_meta, *, batch, seq, rows):
    steps = seq // rows
    blk = pl.BlockSpec((rows, D_RET), lambda b, t: (b * steps + t, 0))
    head_tab = pltpu.VMEM((RET_HEADS, CHUNK, CHUNK), f32)
    return pl.pallas_call(
        functools.partial(_retention_body, rows=rows),
        grid=(batch, steps),
        in_specs=[
            _resident((RET_HEADS, 1, LANES)),
            blk, blk, blk, blk,
            _resident((1, D_RET)),
            _resident((N_META, D_RET)),
            _resident((N_META, D_RET)),
        ],
        out_specs=blk,
        out_shape=jax.ShapeDtypeStruct((batch * seq, D_RET), bf16),
        scratch_shapes=[head_tab, head_tab, head_tab,
                        pltpu.VMEM((RET_HEADS, RET_DIM, RET_DIM), f32)],
        compiler_params=pltpu.CompilerParams(
            dimension_semantics=("parallel", "arbitrary"), vmem_limit_bytes=VMEM_LIMIT),
        name="retention",
    )(log_g, rq, rk, rv, rg, gn_w, rk_meta, rv_meta)


def _mla_body(qt_ref, k_ref, vt_ref, km_ref, vmt_ref, *rest, tq, hg, n_casts):
    cast_src, o_ref, cast_dst = rest[:n_casts], rest[n_casts], rest[n_casts + 1:2 * n_casts + 1]
    s_ref, m_ref, l_ref, acc_ref = rest[2 * n_casts + 1:]
    qi = pl.program_id(2)
    heads = range(hg)

    def qk_cols(g):
        return slice(g * MLA_QK_PAD, (g + 1) * MLA_QK_PAD)

    def v_cols(g):
        return slice(g * MLA_V, (g + 1) * MLA_V)

    def side_casts():
        for src, dst in zip(cast_src, cast_dst):
            dst[...] = src[...].astype(bf16)

    def meta_scores():
        return [_dot(km_ref[:, qk_cols(g)], qt_ref[qk_cols(g), :]) for g in heads]

    def meta_keys(s_all):
        for g in heads:
            s = s_all[g]
            m0 = jnp.max(s, axis=0, keepdims=True)
            p = jnp.exp2(s - m0)
            m_ref[g] = m0
            l_ref[g] = jnp.sum(p, axis=0, keepdims=True)
            acc_ref[g] = _dot(vmt_ref[v_cols(g), :], p.astype(bf16))

    def head_scores(j, slot, g):
        r = pl.ds(pl.multiple_of(j * tq, tq), tq)
        s_ref[slot, g] = _dot(k_ref[r, qk_cols(g)], qt_ref[qk_cols(g), :])

    def scores(j, slot):
        for g in heads:
            head_scores(j, slot, g)

    def head_block(j, slot, g, masked):
        s = s_ref[slot, g]
        if masked:
            kpos = lax.broadcasted_iota(jnp.int32, (tq, tq), 0)
            qpos = lax.broadcasted_iota(jnp.int32, (tq, tq), 1)
            s = jnp.where(kpos <= qpos, s, -1e30)
        m_old = m_ref[g]
        m_new = jnp.maximum(m_old, jnp.max(s, axis=0, keepdims=True))
        alpha = jnp.exp2(m_old - m_new)
        p = jnp.exp2(s - m_new)
        l_ref[g] = alpha * l_ref[g] + jnp.sum(p, axis=0, keepdims=True)
        acc_ref[g] = alpha * acc_ref[g] + _dot(vt_ref[j, v_cols(g), :], p.astype(bf16))
        m_ref[g] = m_new

    def block(j, slot, masked):
        for g in heads:
            head_block(j, slot, g, masked)

    def step(j, slot):
        lead = min(2, hg)
        for g in range(lead):
            head_scores(j + 1, 1 - slot, g)
        for g in heads:
            head_block(j, slot, g, masked=False)
            if g + lead < hg:
                head_scores(j + 1, 1 - slot, g + lead)

    def pair(i, carry):
        step(2 * i, 0)
        step(2 * i + 1, 1)
        return carry

    s_meta = meta_scores()
    scores(0, 0)
    meta_keys(s_meta)
    lax.fori_loop(0, lax.shift_right_logical(qi, 1), pair, 0)
    odd = (qi & 1) == 1

    @pl.when(odd)
    def _():
        side_casts()
        step(qi - 1, 0)
        block(qi, 1, masked=True)

    @pl.when(jnp.logical_not(odd))
    def _():
        side_casts()
        block(qi, 0, masked=True)

    for g in heads:
        o_ref[:, v_cols(g)] = (acc_ref[g] / l_ref[g]).T.astype(bf16)


def _cast_block_spec(shape, grid):
    steps = int(np.prod(grid))
    rows, cols = shape
    for col_parts in (1, 2, 4, 8, 16):
        row_parts = steps // col_parts
        if (steps % col_parts == 0 and rows % (16 * row_parts) == 0
                and cols % (LANES * col_parts) == 0):
            break
    else:
        raise ValueError(f"no per-step blocking of {shape} over {steps} grid steps")

    def index_map(*idx):
        step = 0
        for i, n in zip(idx, grid):
            step = step * n + i
        return step // col_parts, step % col_parts

    return pl.BlockSpec((rows // row_parts, cols // col_parts), index_map)


def _mla(qt, kc, vt, kc_meta, vt_meta, weights_to_cast=(), *, batch, seq, hg):
    tq = vt.shape[-1]
    nq = seq // tq
    grid = (batch, MLA_HEADS // hg, nq)
    cast_specs = [_cast_block_spec(w.shape, grid) for w in weights_to_cast]
    return pl.pallas_call(
        functools.partial(_mla_body, tq=tq, hg=hg, n_casts=len(cast_specs)),
        grid=grid,
        in_specs=[
            pl.BlockSpec((hg * MLA_QK_PAD, tq), lambda b, h, i: (h, b * nq + i)),
            pl.BlockSpec((seq, hg * MLA_QK_PAD), lambda b, h, i: (b, h)),
            pl.BlockSpec((nq, hg * MLA_V, tq), lambda b, h, i: (b, h, 0)),
            pl.BlockSpec((N_META, hg * MLA_QK_PAD), lambda b, h, i: (0, h)),
            pl.BlockSpec((hg * MLA_V, N_META), lambda b, h, i: (h, 0)),
        ] + cast_specs,
        out_specs=[pl.BlockSpec((tq, hg * MLA_V), lambda b, h, i: (b * nq + i, h))] + cast_specs,
        out_shape=[jax.ShapeDtypeStruct((batch * seq, D_MLA), bf16)] + [
            jax.ShapeDtypeStruct(w.shape, bf16) for w in weights_to_cast],
        scratch_shapes=[
            pltpu.VMEM((2, hg, tq, tq), f32),
            pltpu.VMEM((hg, 1, tq), f32),
            pltpu.VMEM((hg, 1, tq), f32),
            pltpu.VMEM((hg, MLA_V, tq), f32),
        ],
        compiler_params=pltpu.CompilerParams(
            dimension_semantics=("parallel", "parallel", "arbitrary"),
            vmem_limit_bytes=VMEM_LIMIT),
        name="mla_attention",
    )(qt, kc, vt, kc_meta, vt_meta, *weights_to_cast)


def _mix_out_body(ret_ref, mla_ref, wo_ref, h_ref, post_ref, npre_ref, o_ref, xo_ref, *, ts):
    for r in range(h_ref.shape[0] // ts):
        rows = slice(r * ts, (r + 1) * ts)
        mixed = jnp.concatenate([ret_ref[rows, :], mla_ref[rows, :]], axis=1)
        m = _dot(mixed, wo_ref[...])
        hn = h_ref[rows, :] + _rms(m, post_ref[...])
        o_ref[rows, :] = hn
        xo_ref[rows, :] = _rms(hn, npre_ref[...]).astype(bf16)


def _mix_out(ret, mla, wo, h, post_w, next_pre_w, *, tm, ts):
    n = h.shape[0]
    row = pl.BlockSpec((tm, D_MODEL), lambda i: (i, 0))
    return pl.pallas_call(
        functools.partial(_mix_out_body, ts=ts),
        grid=(n // tm,),
        in_specs=[
            pl.BlockSpec((tm, D_RET), lambda i: (i, 0)),
            pl.BlockSpec((tm, D_MLA), lambda i: (i, 0)),
            _resident(wo.shape),
            row,
            _resident((1, D_MODEL)),
            _resident((1, D_MODEL)),
        ],
        out_specs=[row, row],
        out_shape=[jax.ShapeDtypeStruct((n, D_MODEL), f32),
                   jax.ShapeDtypeStruct((n, D_MODEL), bf16)],
        compiler_params=pltpu.CompilerParams(
            dimension_semantics=("parallel",), vmem_limit_bytes=VMEM_LIMIT),
        name="mix_out",
    )(ret, mla, wo, h, post_w, next_pre_w)


def _rope_tables(seq_len):
    pos = np.arange(seq_len, dtype=np.float64)

    def cos_sin(dim):
        inv = ROPE_THETA ** (-np.arange(0, dim, 2, dtype=np.float64) / dim)
        ang = pos[:, None] * inv[None, :]
        return np.cos(ang), np.sin(ang)

    c, s = cos_sin(RET_DIM)
    cr = np.concatenate([c, c], axis=-1)
    sr = np.concatenate([-s, s], axis=-1)
    c, s = cos_sin(MLA_ROPE)
    z = np.zeros_like(c)
    cm = np.concatenate([c, z, c, z], axis=-1)
    sm = np.concatenate([-s, z, s, z], axis=-1)
    return tuple(t.astype(np.float32) for t in (cr, sr, cm, sm))


def _pad_rope_cols(w):
    half = MLA_ROPE // 2
    z = jnp.zeros(w.shape[:-1] + (half,), w.dtype)
    return jnp.concatenate([w[..., :half], z, w[..., half:], z], axis=-1)


def kernel(x, meta_tokens, ffn1_pre_norm, ffn1_w_gate, ffn1_w_up, ffn1_w_down, ffn1_post_norm, mix_pre_norm, w_in, ret_group_norm, mla_q_norm, mla_w_uq, mla_kv_norm, mla_w_uk, mla_w_uv, w_out, mix_post_norm, ffn2_pre_norm, ffn2_w_gate, ffn2_w_up, ffn2_w_down, ffn2_post_norm):
    batch, seq, _ = x.shape
    assert ffn1_pre_norm.shape[0] == 1, "single-layer trunk only"
    l = 0
    cr, sr, cm, sm = _rope_tables(N_META + seq)
    tabs_meta = tuple(t[:N_META] for t in (cr, sr, cm, sm)) + (cm[:N_META].T, sm[:N_META].T)
    tabs_real = tuple(t[N_META:] for t in (cr, sr, cm, sm)) + (cm[N_META:].T, sm[N_META:].T)
    log_g = jnp.log(1.0 - 2.0 ** (-5.0 - jnp.arange(RET_HEADS, dtype=f32)))
    log_g = jnp.broadcast_to(log_g[:, None, None], (RET_HEADS, 1, LANES))

    h = x.reshape(batch * seq, D_MODEL)
    hm = meta_tokens.astype(x.dtype)
    vec = lambda w: w[l].reshape(1, -1)
    tf = 512
    ffn1 = (vec(ffn1_pre_norm), ffn1_w_gate[l].astype(bf16), ffn1_w_up[l].astype(bf16),
            ffn1_w_down[l].astype(bf16), vec(ffn1_post_norm))
    wkr = _pad_rope_cols(w_in[l][:, _OFF_KR:]).astype(bf16)
    wuq = mla_w_uq[l].reshape(MLA_Q_RANK, MLA_HEADS, MLA_NOPE + MLA_ROPE)
    wuq = jnp.concatenate([wuq[..., :MLA_NOPE], _pad_rope_cols(wuq[..., MLA_NOPE:])], axis=-1)
    wuqt = wuq.reshape(MLA_Q_RANK, MLA_HEADS * MLA_QK_PAD).T.astype(bf16)

    h, xn, win = _ffn(h, *ffn1, vec(mix_pre_norm), w_in[l], tm=1024, ts=256, tf=tf,
                      vmem_limit=VMEM_LIMIT_WIDE)
    mix_w = (win, wkr, vec(mla_q_norm), wuqt, vec(mla_kv_norm),
             mla_w_uk[l].astype(bf16), mla_w_uv[l].T.astype(bf16))
    _, xn_m = _ffn(hm, *ffn1, vec(mix_pre_norm), tm=N_META, ts=N_META, tf=tf)
    _, rk_m, rv_m, _, _, kc_m, vt_m = _mix_in(xn_m, *mix_w, *tabs_meta, tm=N_META, ts=N_META)
    rq, rk, rv, rg, qt, kc, vt = _mix_in(xn, *mix_w, *tabs_real, tm=512, ts=256)
    ret = _retention(log_g, rq, rk, rv, rg, vec(ret_group_norm), rk_m, rv_m,
                     batch=batch, seq=seq, rows=512)
    later_weights = (w_out[l], ffn2_w_gate[l], ffn2_w_up[l], ffn2_w_down[l])
    mla, wo, wg2, wu2, wd2 = _mla(qt, kc, vt, kc_m, vt_m[0], later_weights,
                                  batch=batch, seq=seq, hg=4)
    h, xn = _mix_out(ret, mla, wo, h, vec(mix_post_norm), vec(ffn2_pre_norm), tm=512, ts=128)
    h = _ffn(h, xn, wg2, wu2, wd2, vec(ffn2_post_norm), tm=1024, ts=256, tf=tf)
    return h.reshape(batch, seq, D_MODEL)
```
